```python
import math
import jax, jax.numpy as jnp
from jax import lax
import numpy as np

D_MODEL = 1024
BATCH = 8
SEQ = 4096
DEPTH = 1

CHUNK = 64
EPS = 1e-6
S5_WIDTH = 512
S5_GROUP = 16
S5_GROUPS = S5_WIDTH // S5_GROUP
S5_STATE = 64
SSD_HEADDIM = 64
SSD_HEADS = 16
SSD_WIDTH = SSD_HEADS * SSD_HEADDIM
SSD_GROUPS = 4
SSD_HEADS_PER_GROUP = SSD_HEADS // SSD_GROUPS
SSD_STATE = 128
SSD_CONV = 4
SSD_CONV_DIM = SSD_WIDTH + 2 * SSD_GROUPS * SSD_STATE
D_FF = 4 * D_MODEL
SPLIT_SIZES = (S5_WIDTH, SSD_WIDTH, SSD_CONV_DIM, SSD_HEADS, D_MODEL, D_MODEL)
SPLIT_POINTS = tuple(int(v) for v in np.cumsum(SPLIT_SIZES)[:-1])
IN_PROJ_DIM = int(sum(SPLIT_SIZES))

kernel_name = "hybrid_s5_ssd_gated_block"


def rmsnorm(x, g):
    x32 = x.astype(jnp.float32)
    y = x32 * lax.rsqrt(jnp.mean(x32 * x32, axis=-1, keepdims=True) + EPS)
    return (y * g.astype(jnp.float32)).astype(x.dtype)


def _complex_linear_combine(left, right):
    ar1, ai1, br1, bi1 = left
    ar2, ai2, br2, bi2 = right
    ar = ar2 * ar1 - ai2 * ai1
    ai = ar2 * ai1 + ai2 * ar1
    br = ar2 * br1 - ai2 * bi1 + br2
    bi = ar2 * bi1 + ai2 * br1 + bi2
    return (ar, ai, br, bi)


def s5_mixer(u, a_re, a_im, log_dt, b_re, b_im, c_re, c_im, d, glu_w, glu_b):
    bsz, L, _ = u.shape
    ug = u.reshape(bsz, L, S5_GROUPS, S5_GROUP)
    dt = jnp.exp(log_dt)[:, None]
    mag = jnp.exp(a_re * dt)
    ang = a_im * dt
    lb_re = mag * jnp.cos(ang)
    lb_im = mag * jnp.sin(ang)
    nr = lb_re - 1.0
    ni = lb_im
    den = a_re * a_re + a_im * a_im
    cr = (nr * a_re + ni * a_im) / den
    ci = (ni * a_re - nr * a_im) / den
    bb_re = cr[..., None] * b_re - ci[..., None] * b_im
    bb_im = cr[..., None] * b_im + ci[..., None] * b_re
    bu_re = jnp.einsum("blgh,gph->blgp", ug, bb_re)
    bu_im = jnp.einsum("blgh,gph->blgp", ug, bb_im)
    a_re_seq = jnp.broadcast_to(lb_re[None, None], (1, L, S5_GROUPS, S5_STATE))
    a_im_seq = jnp.broadcast_to(lb_im[None, None], (1, L, S5_GROUPS, S5_STATE))
    _, _, s_re, s_im = lax.associative_scan(
        _complex_linear_combine, (a_re_seq, a_im_seq, bu_re, bu_im), axis=1)
    y = (jnp.einsum("blgp,ghp->blgh", s_re, c_re)
         - jnp.einsum("blgp,ghp->blgh", s_im, c_im)
         + d * ug)
    y = jax.nn.gelu(y.reshape(bsz, L, S5_WIDTH))
    return y * jax.nn.sigmoid(y @ glu_w + glu_b)


def causal_depthwise_conv(u, w, b):
    k = w.shape[0]
    out = lax.conv_general_dilated(
        u, w[:, None, :], window_strides=(1,), padding=[(k - 1, 0)],
        dimension_numbers=("NWC", "WIO", "NWC"), feature_group_count=u.shape[-1])
    return out + b


def segsum(x):
    t = x.shape[-1]
    xe = jnp.broadcast_to(x[..., None], x.shape + (t,))
    xe = jnp.where(jnp.tril(jnp.ones((t, t), dtype=bool), -1), xe, 0)
    cs = jnp.cumsum(xe, axis=-2)
    return jnp.where(jnp.tril(jnp.ones((t, t), dtype=bool), 0), cs, -jnp.inf)


def ssd_mixer(z, xbc, dt_raw, conv_w, conv_b, dt_bias, a_log, d_skip, norm_g):
    bsz, L, _ = z.shape
    nc = L // CHUNK
    G, E, P, N = SSD_GROUPS, SSD_HEADS_PER_GROUP, SSD_HEADDIM, SSD_STATE
    xbc = jax.nn.silu(causal_depthwise_conv(xbc, conv_w, conv_b))
    xs, bm, cm = jnp.split(xbc, [SSD_WIDTH, SSD_WIDTH + G * N], axis=-1)
    dt = jax.nn.softplus(dt_raw + dt_bias)
    a = -jnp.exp(a_log)
    xh = xs.reshape(bsz, L, SSD_HEADS, P)
    X = (xh * dt[..., None]).reshape(bsz, nc, CHUNK, G, E, P)
    dA = (dt * a).reshape(bsz, nc, CHUNK, G, E).transpose(0, 3, 4, 1, 2)
    Bc = bm.reshape(bsz, nc, CHUNK, G, N)
    Cc = cm.reshape(bsz, nc, CHUNK, G, N)
    a_cum = jnp.cumsum(dA, axis=-1)
    Lmat = jnp.exp(segsum(dA))
    CB = jnp.einsum("bclgn,bcsgn->bcgls", Cc, Bc)
    y_diag = jnp.einsum("bcgls,bgecls,bcsgep->bclgep", CB, Lmat, X)
    decay_states = jnp.exp(a_cum[..., -1:] - a_cum)
    states = jnp.einsum("bclgn,bgecl,bclgep->bcgepn", Bc, decay_states, X)
    states = jnp.concatenate([jnp.zeros_like(states[:, :1]), states], axis=1)
    chunk_decay = jnp.pad(a_cum[..., -1], ((0, 0), (0, 0), (0, 0), (1, 0)))
    decay_chunk = jnp.exp(segsum(chunk_decay))
    states = jnp.einsum("bgezc,bcgepn->bzgepn", decay_chunk, states)[:, :-1]
    y_off = jnp.einsum("bclgn,bcgepn,bgecl->bclgep", Cc, states, jnp.exp(a_cum))
    y = (y_diag + y_off).reshape(bsz, L, SSD_HEADS, P) + xh * d_skip[:, None]
    y = y.reshape(bsz, L, SSD_WIDTH)
    return rmsnorm(y * jax.nn.silu(z), norm_g)


def setup_inputs(seed: int = 0) -> dict:
    key = jax.random.key(seed)
    ks = jax.random.split(key, 32)
    f32 = jnp.float32
    nrm = lambda k, shape, s: jax.random.normal(k, shape, f32) * s
    x = jax.random.normal(ks[0], (BATCH, SEQ, D_MODEL), f32)
    norm_mix = 1.0 + nrm(ks[1], (DEPTH, D_MODEL), 0.02)
    w_in = nrm(ks[2], (DEPTH, D_MODEL, IN_PROJ_DIM), D_MODEL ** -0.5)
    s5_a_re = -0.5 + nrm(ks[3], (DEPTH, S5_GROUPS, S5_STATE), 0.01)
    s5_a_im = (jnp.pi * jnp.arange(S5_STATE, dtype=f32))[None, None, :] + nrm(ks[4], (DEPTH, S5_GROUPS, S5_STATE), 0.01)
    s5_log_dt = jax.random.uniform(ks[5], (DEPTH, S5_GROUPS), f32, math.log(1e-3), math.log(1e-1))
    s5_b_re = nrm(ks[6], (DEPTH, S5_GROUPS, S5_STATE, S5_GROUP), (2.0 * S5_GROUP) ** -0.5)
    s5_b_im = nrm(ks[7], (DEPTH, S5_GROUPS, S5_STATE, S5_GROUP), (2.0 * S5_GROUP) ** -0.5)
    s5_c_re = nrm(ks[8], (DEPTH, S5_GROUPS, S5_GROUP, S5_STATE), S5_STATE ** -0.5)
    s5_c_im = nrm(ks[9], (DEPTH, S5_GROUPS, S5_GROUP, S5_STATE), S5_STATE ** -0.5)
    s5_d = nrm(ks[10], (DEPTH, S5_GROUPS, S5_GROUP), 1.0)
    s5_glu_w = nrm(ks[11], (DEPTH, S5_WIDTH, S5_WIDTH), S5_WIDTH ** -0.5)
    s5_glu_b = nrm(ks[12], (DEPTH, S5_WIDTH), 0.01)
    w_branch_a = nrm(ks[13], (DEPTH, S5_WIDTH, D_MODEL), S5_WIDTH ** -0.5)
    ssd_conv_w = nrm(ks[14], (DEPTH, SSD_CONV, SSD_CONV_DIM), SSD_CONV ** -0.5)
    ssd_conv_b = nrm(ks[15], (DEPTH, SSD_CONV_DIM), 0.01)
    dt0 = jnp.exp(jax.random.uniform(ks[16], (DEPTH, SSD_HEADS), f32, math.log(1e-3), math.log(1e-1)))
    ssd_dt_bias = dt0 + jnp.log(-jnp.expm1(-dt0))
    ssd_a_log = jnp.log(jax.random.uniform(ks[17], (DEPTH, SSD_HEADS), f32, 1.0, 16.0))
    ssd_d = 1.0 + nrm(ks[18], (DEPTH, SSD_HEADS), 0.02)
    ssd_norm = 1.0 + nrm(ks[19], (DEPTH, SSD_WIDTH), 0.02)
    w_branch_b = nrm(ks[20], (DEPTH, SSD_WIDTH, D_MODEL), SSD_WIDTH ** -0.5)
    w_out = nrm(ks[21], (DEPTH, D_MODEL, D_MODEL), D_MODEL ** -0.5)
    norm_mlp = 1.0 + nrm(ks[22], (DEPTH, D_MODEL), 0.02)
    w_ff1 = nrm(ks[23], (DEPTH, D_MODEL, D_FF), D_MODEL ** -0.5)
    w_ff2 = nrm(ks[24], (DEPTH, D_FF, D_MODEL), D_FF ** -0.5)
    norm_final = 1.0 + nrm(ks[25], (D_MODEL,), 0.02)
    return {"x": x, "norm_mix": norm_mix, "w_in": w_in,
            "s5_a_re": s5_a_re, "s5_a_im": s5_a_im, "s5_log_dt": s5_log_dt,
            "s5_b_re": s5_b_re, "s5_b_im": s5_b_im, "s5_c_re": s5_c_re, "s5_c_im": s5_c_im,
            "s5_d": s5_d, "s5_glu_w": s5_glu_w, "s5_glu_b": s5_glu_b, "w_branch_a": w_branch_a,
            "ssd_conv_w": ssd_conv_w, "ssd_conv_b": ssd_conv_b, "ssd_dt_bias": ssd_dt_bias,
            "ssd_a_log": ssd_a_log, "ssd_d": ssd_d, "ssd_norm": ssd_norm, "w_branch_b": w_branch_b,
            "w_out": w_out, "norm_mlp": norm_mlp, "w_ff1": w_ff1, "w_ff2": w_ff2,
            "norm_final": norm_final}


def reference(x, norm_mix, w_in, s5_a_re, s5_a_im, s5_log_dt, s5_b_re, s5_b_im, s5_c_re, s5_c_im,
              s5_d, s5_glu_w, s5_glu_b, w_branch_a, ssd_conv_w, ssd_conv_b, ssd_dt_bias, ssd_a_log,
              ssd_d, ssd_norm, w_branch_b, w_out, norm_mlp, w_ff1, w_ff2, norm_final):
    for i in range(DEPTH):
        h = rmsnorm(x, norm_mix[i])
        proj = h @ w_in[i]
        u_s5, z, xbc, dt_raw, g_a, g_b = jnp.split(proj, SPLIT_POINTS, axis=-1)
        y_a = s5_mixer(u_s5, s5_a_re[i], s5_a_im[i], s5_log_dt[i], s5_b_re[i], s5_b_im[i],
                       s5_c_re[i], s5_c_im[i], s5_d[i], s5_glu_w[i], s5_glu_b[i]) @ w_branch_a[i]
        y_b = ssd_mixer(z, xbc, dt_raw, ssd_conv_w[i], ssd_conv_b[i], ssd_dt_bias[i],
                        ssd_a_log[i], ssd_d[i], ssd_norm[i]) @ w_branch_b[i]
        merged = jax.nn.sigmoid(g_a) * y_a + jax.nn.sigmoid(g_b) * y_b
        x = x + merged @ w_out[i]
        h = rmsnorm(x, norm_mlp[i])
        x = x + jnp.square(jax.nn.relu(h @ w_ff1[i])) @ w_ff2[i]
    return rmsnorm(x, norm_final)
```

```python
import functools
import math

import jax
import jax.numpy as jnp
from jax import lax
from jax.experimental import pallas as pl
from jax.experimental.pallas import tpu as pltpu

F32 = jnp.float32
BF16 = jnp.bfloat16

EPS = 1e-6
LANES = 128
SUBLANES = 8

S5_GROUP = 16
S5_STATE = 64
S5_Q = 8
S5_GROUPS_PER_TILE = LANES // S5_GROUP
S5_TT = 512

SSD_HEADDIM = 64
SSD_STATE = 128
SSD_CONV = 4
SSD_Q = 128
SSD_TT = 512
NEG_BIG = -1e30

TOKEN_TILE = 512
VMEM_LIMIT = 56 * 1024 * 1024


def _rms(x, g):
    return x * lax.rsqrt(jnp.mean(x * x, axis=-1, keepdims=True) + EPS) * g


def _sigmoid(x):
    return 1.0 / (1.0 + jnp.exp(-x))


def _dot(a, b):
    return jnp.dot(a, b, preferred_element_type=F32)


def _inproj_kernel(x_ref, g_ref, w_ref, u_ref, z_ref, xbc_ref, ga_ref, gb_ref, dt_ref):
    hb = _rms(x_ref[...], g_ref[...]).astype(BF16)
    lo = 0
    for ref in (u_ref, z_ref, xbc_ref, ga_ref, gb_ref, dt_ref):
        n = ref.shape[-1]
        ref[...] = _dot(hb, w_ref[:, lo:lo + n])
        lo += n


def _in_proj(x2, g, w, widths):
    t, d = x2.shape
    tm = TOKEN_TILE
    row = lambda n: pl.BlockSpec((tm, n), lambda i: (i, 0))
    const = lambda shape: pl.BlockSpec(shape, lambda i: (0, 0), pipeline_mode=pl.Buffered(1))
    return pl.pallas_call(
        _inproj_kernel,
        grid=(t // tm,),
        in_specs=[row(d), const((1, d)), const(w.shape)],
        out_specs=[row(n) for n in widths],
        out_shape=[jax.ShapeDtypeStruct((t, n), F32) for n in widths],
        compiler_params=pltpu.CompilerParams(
            dimension_semantics=("arbitrary",), vmem_limit_bytes=VMEM_LIMIT),
        name="in_proj",
    )(x2, g, w)


def _s5_operators(a_re, a_im, log_dt, b_re, b_im, c_re, c_im):
    q = S5_Q
    g, p = a_re.shape
    hp = lax.Precision.HIGHEST
    dt = jnp.exp(log_dt)[:, None]
    steps = jnp.arange(q + 1, dtype=F32)[:, None, None]
    mag = jnp.exp(a_re * dt * steps)
    ang = a_im * dt * steps
    pw_re = mag * jnp.cos(ang)
    pw_im = mag * jnp.sin(ang)
    nr = pw_re[1] - 1.0
    ni = pw_im[1]
    den = a_re * a_re + a_im * a_im
    cr = (nr * a_re + ni * a_im) / den
    ci = (ni * a_re - nr * a_im) / den
    bb_re = cr[..., None] * b_re - ci[..., None] * b_im
    bb_im = cr[..., None] * b_im + ci[..., None] * b_re
    cp_re = c_re[None] * pw_re[:, :, None, :] - c_im[None] * pw_im[:, :, None, :]
    cp_im = c_re[None] * pw_im[:, :, None, :] + c_im[None] * pw_re[:, :, None, :]
    kj = (jnp.einsum("jghp,gpk->jghk", cp_re[:q], bb_re, precision=hp)
          - jnp.einsum("jghp,gpk->jghk", cp_im[:q], bb_im, precision=hp))
    tprime = jnp.arange(q)[:, None]
    jj = jnp.arange(q)[None, :]
    lag = tprime - jj
    toep = jnp.where((lag >= 0)[:, :, None, None, None], kj[jnp.clip(lag, 0, q - 1)], 0.0)
    nt = g // S5_GROUPS_PER_TILE
    g8 = S5_GROUPS_PER_TILE
    eye = jnp.eye(g8, dtype=F32)
    hh = S5_GROUP
    toep = toep.reshape(q, q, nt, g8, hh, hh)
    wi = jnp.einsum("tjlahk,ab->ljaktbh", toep, eye).reshape(nt, q * LANES, q * LANES)
    rev_steps = (q - 1) - steps[:q]
    rev_mag = jnp.exp(a_re * dt * rev_steps)
    rev_ang = a_im * dt * rev_steps
    rev_re = (rev_mag * jnp.cos(rev_ang))[..., None]
    rev_im = (rev_mag * jnp.sin(rev_ang))[..., None]
    sb_re = (rev_re * bb_re[None] - rev_im * bb_im[None]).reshape(q, nt, g8, p, hh)
    sb_im = (rev_re * bb_im[None] + rev_im * bb_re[None]).reshape(q, nt, g8, p, hh)
    sb = jnp.stack([sb_re, sb_im], axis=0)
    ws = jnp.einsum("sjlapk,ab->ljaksbp", sb, eye).reshape(nt, q * LANES, 2 * g8 * p)
    co = jnp.stack([cp_re[1:], -cp_im[1:]], axis=0).reshape(2, q, nt, g8, hh, p)
    wo = jnp.einsum("stlahp,ab->lsaptbh", co, eye).reshape(nt, 2 * g8 * p, q * LANES)
    are = pw_re[q].reshape(nt, 1, g8 * p)
    aim = pw_im[q].reshape(nt, 1, g8 * p)
    return wi.astype(BF16), ws.astype(BF16), wo.astype(BF16), are, aim


def _s5_kernel(u_ref, wi_ref, ws_ref, wo_ref, are_ref, aim_ref, d_ref, y_ref,
               sre_ref, sim_ref, v_ref, sp_ref):
    q = S5_Q
    nb, tt, _ = u_ref.shape
    c = tt // q
    r = nb * c
    half = sre_ref.shape[-1]

    @pl.when(pl.program_id(1) == 0)
    def _():
        sre_ref[...] = jnp.zeros_like(sre_ref)
        sim_ref[...] = jnp.zeros_like(sim_ref)

    us = [u_ref[:, pl.ds(j, c, stride=q), :].reshape(r, LANES) for j in range(q)]
    a = jnp.concatenate([uj.astype(BF16) for uj in us], axis=1)

    v = _dot(a, ws_ref[0])
    slabs = half // LANES
    for n in range(2 * slabs):
        v_ref[n] = v[:, n * LANES:(n + 1) * LANES]

    are = jnp.broadcast_to(are_ref[0], (nb, half))
    aim = jnp.broadcast_to(aim_ref[0], (nb, half))
    s_re = sre_ref[...]
    s_im = sim_ref[...]
    for k in range(c):
        step_rows = pl.ds(k, nb, stride=c)
        v_re = jnp.concatenate([v_ref[n, step_rows, :] for n in range(slabs)], axis=1)
        v_im = jnp.concatenate([v_ref[slabs + n, step_rows, :] for n in range(slabs)], axis=1)
        for n in range(slabs):
            sp_ref[n, step_rows, :] = s_re[:, n * LANES:(n + 1) * LANES]
            sp_ref[slabs + n, step_rows, :] = s_im[:, n * LANES:(n + 1) * LANES]
        n_re = are * s_re - aim * s_im + v_re
        n_im = are * s_im + aim * s_re + v_im
        s_re, s_im = n_re, n_im
    sre_ref[...] = s_re
    sim_ref[...] = s_im

    spb = jnp.concatenate([sp_ref[n] for n in range(2 * slabs)], axis=1).astype(BF16)
    d = d_ref[0]
    mxu = 2 * LANES
    steps_per_tile = mxu // LANES
    for n in range(q * LANES // mxu):
        kk = (n + 1) * mxu
        cols = slice(n * mxu, (n + 1) * mxu)
        yn = _dot(a[:, :kk], wi_ref[0, :kk, cols]) + _dot(spb, wo_ref[0, :, cols])
        for s in range(steps_per_tile):
            j = n * steps_per_tile + s
            yj = yn[:, s * LANES:(s + 1) * LANES] + d * us[j]
            y_ref[:, pl.ds(j, c, stride=q), :] = yj.reshape(nb, c, LANES)


def _s5_branch(u3, wi, ws, wo, are, aim, d):
    nb, seq, width = u3.shape
    nt = width // LANES
    tt = S5_TT
    kq = S5_Q * LANES
    half = are.shape[-1]
    r = nb * tt // S5_Q
    tile = pl.BlockSpec((nb, tt, LANES), lambda l, t: (0, t, l))
    per_tile = lambda shape: pl.BlockSpec((1,) + shape, lambda l, t: (l, 0, 0))
    return pl.pallas_call(
        _s5_kernel,
        grid=(nt, seq // tt),
        in_specs=[tile, per_tile((kq, kq)), per_tile((kq, 2 * half)), per_tile((2 * half, kq)),
                  per_tile((1, half)), per_tile((1, half)), per_tile((1, LANES))],
        out_specs=tile,
        out_shape=jax.ShapeDtypeStruct(u3.shape, F32),
        scratch_shapes=[pltpu.VMEM((nb, half), F32), pltpu.VMEM((nb, half), F32),
                        pltpu.VMEM((2 * half // LANES, r, LANES), F32),
                        pltpu.VMEM((2 * half // LANES, r, LANES), F32)],
        compiler_params=pltpu.CompilerParams(
            dimension_semantics=("arbitrary", "arbitrary"), vmem_limit_bytes=VMEM_LIMIT),
        name="s5_branch",
    )(u3, wi, ws, wo, are, aim, d)


def _split3(x):
    hi = x.astype(BF16)
    r1 = x - hi.astype(F32)
    mid = r1.astype(BF16)
    lo = (r1 - mid.astype(F32)).astype(BF16)
    return hi, mid, lo


def _ssd_kernel(xbc_ref, z_ref, dt_ref, cw_ref, cb_ref, dtb_ref, a_ref, dskip_ref, ng_ref,
                y_ref, halo_ref, pad_ref, xc_ref, dts_ref, ys_ref, st_ref):
    tt, width = z_ref.shape
    q = SSD_Q
    n_state = SSD_STATE
    pairs = width // LANES
    halo = SUBLANES
    b_off = width
    c_off = width + (xc_ref.shape[-1] - width) // 2

    @pl.when(pl.program_id(1) == 0)
    def _():
        halo_ref[...] = jnp.zeros_like(halo_ref)
        st_ref[...] = jnp.zeros_like(st_ref)

    pad_ref[0:halo, :] = halo_ref[...]
    pad_ref[halo:, :] = xbc_ref[...]
    halo_ref[...] = xbc_ref[tt - halo:tt, :]
    for blk in range(tt // q):
        r0 = blk * q
        acc = jnp.broadcast_to(cb_ref[...], (q, cb_ref.shape[-1]))
        for k in range(SSD_CONV):
            s0 = r0 + halo - (SSD_CONV - 1) + k
            acc = acc + cw_ref[k:k + 1, :] * pad_ref[s0:s0 + q, :]
        xc_ref[r0:r0 + q, :] = acc * _sigmoid(acc)
        v = dt_ref[r0:r0 + q, :] + dtb_ref[...]
        dts_ref[r0:r0 + q, :] = jnp.maximum(v, 0.0) + jnp.log1p(jnp.exp(-jnp.abs(v)))

    row_i = lax.broadcasted_iota(jnp.int32, (q, q), 0)
    col_i = lax.broadcasted_iota(jnp.int32, (q, q), 1)
    causal = row_i >= col_i
    tri = causal.astype(BF16)
    lane = lax.broadcasted_iota(jnp.int32, (q, LANES), 1)
    first_head = lane < SSD_HEADDIM
    lane1 = lax.broadcasted_iota(jnp.int32, (1, LANES), 1)
    first_head1 = lane1 < SSD_HEADDIM

    def chunk(ci, carry):
        r0 = pl.multiple_of(ci * q, q)
        rows = pl.ds(r0, q)
        dtq = dts_ref[rows, :]
        da = dtq * a_ref[...]
        hi, mid, lo = _split3(da)
        acum = _dot(tri, hi) + _dot(tri, mid) + _dot(tri, lo)
        acum_t = acum.T
        dt_t = dtq.T
        w_t = jnp.exp(acum_t[:, q - 1:q] - acum_t) * dt_t
        for g in range(pairs // 2):
            bg = xc_ref[rows, b_off + g * n_state:b_off + (g + 1) * n_state]
            cg = xc_ref[rows, c_off + g * n_state:c_off + (g + 1) * n_state].astype(BF16)
            cb = lax.dot_general(cg, bg.astype(BF16), (((1,), (1,)), ((), ())),
                                 preferred_element_type=F32)
            bg_t = bg.T
            for m in (2 * g, 2 * g + 1):
                xs = xc_ref[rows, m * LANES:(m + 1) * LANES]
                mats, wts, ecols = [], [], []
                for h in (2 * m, 2 * m + 1):
                    col = jnp.broadcast_to(acum[:, h:h + 1], (q, q))
                    rowv = jnp.broadcast_to(acum_t[h:h + 1, :], (q, q))
                    dtrow = jnp.broadcast_to(dt_t[h:h + 1, :], (q, q))
                    decay = jnp.exp(jnp.where(causal, col - rowv, NEG_BIG))
                    mats.append((cb * decay * dtrow).astype(BF16))
                    wts.append((bg_t * jnp.broadcast_to(w_t[h:h + 1, :], (n_state, q))).astype(BF16))
                    ecols.append(jnp.exp(col))
                mcat = jnp.concatenate(mats, axis=1)
                wcat = jnp.concatenate(wts, axis=1)
                xbd = jnp.concatenate([jnp.where(first_head, xs, 0.0),
                                       jnp.where(first_head, 0.0, xs)], axis=0).astype(BF16)
                y_diag = _dot(mcat, xbd)
                st_new = _dot(wcat, xbd)
                st_prev = st_ref[m]
                ecol = jnp.where(first_head, ecols[0], ecols[1])
                y_off = _dot(cg, st_prev.astype(BF16)) * ecol
                cdec = jnp.where(first_head1, ecols[0][q - 1:q, :], ecols[1][q - 1:q, :])
                st_ref[m] = st_prev * cdec + st_new
                ys_ref[rows, m * LANES:(m + 1) * LANES] = (
                    y_diag + y_off + xs * dskip_ref[:, m * LANES:(m + 1) * LANES])
        return carry

    lax.fori_loop(0, tt // q, chunk, 0)

    for blk in range(tt // q):
        r0 = blk * q
        zz = z_ref[r0:r0 + q, :]
        yg = ys_ref[r0:r0 + q, :] * (zz * _sigmoid(zz))
        y_ref[r0:r0 + q, :] = _rms(yg, ng_ref[...])


def _ssd_branch(xbc3, z3, dt3, cw, cb, dtb, a, dskip, ng):
    nb, seq, cdim = xbc3.shape
    width = z3.shape[-1]
    tt = SSD_TT
    tile = lambda n: pl.BlockSpec((None, tt, n), lambda b, t: (b, t, 0))
    const = lambda arr: pl.BlockSpec(arr.shape, lambda b, t: (0, 0))
    return pl.pallas_call(
        _ssd_kernel,
        grid=(nb, seq // tt),
        in_specs=[tile(cdim), tile(width), tile(LANES), const(cw), const(cb), const(dtb),
                  const(a), const(dskip), const(ng)],
        out_specs=tile(width),
        out_shape=jax.ShapeDtypeStruct(z3.shape, F32),
        scratch_shapes=[pltpu.VMEM((SUBLANES, cdim), F32),
                        pltpu.VMEM((tt + SUBLANES, cdim), F32),
                        pltpu.VMEM((tt, cdim), F32),
                        pltpu.VMEM((tt, LANES), F32),
                        pltpu.VMEM((tt, width), F32),
                        pltpu.VMEM((width // LANES, SSD_STATE, LANES), F32)],
        compiler_params=pltpu.CompilerParams(
            dimension_semantics=("arbitrary", "arbitrary"), vmem_limit_bytes=VMEM_LIMIT),
        name="ssd_branch",
    )(xbc3, z3, dt3, cw, cb, dtb, a, dskip, ng)


def _gelu_tanh(x):
    return 0.5 * x * (1.0 + jnp.tanh(math.sqrt(2.0 / math.pi) * (x + 0.044715 * (x * x * x))))


def _tail_kernel(x_ref, y5_ref, yb_ref, ga_ref, gb_ref, gluw_ref, glub_ref, wa_ref, wb_ref,
                 wo_ref, nm_ref, w1_ref, w2_ref, nf_ref, o_ref, *, apply_final):
    ya = _gelu_tanh(y5_ref[...])
    ya = ya * _sigmoid(_dot(ya.astype(BF16), gluw_ref[...]) + glub_ref[...])
    ya = _dot(ya.astype(BF16), wa_ref[...])
    yb = _dot(yb_ref[...].astype(BF16), wb_ref[...])
    merged = _sigmoid(ga_ref[...]) * ya + _sigmoid(gb_ref[...]) * yb
    x1 = x_ref[...] + _dot(merged.astype(BF16), wo_ref[...])
    hb = _rms(x1, nm_ref[...]).astype(BF16)
    acc = x1
    d_ff = w1_ref.shape[-1]
    ff_block = 1024
    for k in range(d_ff // ff_block):
        cols = slice(k * ff_block, (k + 1) * ff_block)
        act = jnp.maximum(_dot(hb, w1_ref[:, cols]), 0.0)
        acc = acc + _dot((act * act).astype(BF16), w2_ref[cols, :])
    if apply_final:
        acc = _rms(acc, nf_ref[...])
    o_ref[...] = acc


def _tail(x2, y5, yb, ga, gb, gluw, glub, wa, wb, wo, nm, w1, w2, nf, apply_final):
    t, d = x2.shape
    tm = TOKEN_TILE
    row = lambda n: pl.BlockSpec((tm, n), lambda i: (i, 0))
    const = lambda arr: pl.BlockSpec(arr.shape, lambda i: (0, 0), pipeline_mode=pl.Buffered(1))
    consts = (gluw, glub, wa, wb, wo, nm, w1, w2, nf)
    return pl.pallas_call(
        functools.partial(_tail_kernel, apply_final=apply_final),
        grid=(t // tm,),
        in_specs=[row(d), row(y5.shape[-1]), row(yb.shape[-1]), row(d), row(d)]
        + [const(c) for c in consts],
        out_specs=row(d),
        out_shape=jax.ShapeDtypeStruct((t, d), F32),
        compiler_params=pltpu.CompilerParams(
            dimension_semantics=("arbitrary",), vmem_limit_bytes=VMEM_LIMIT),
        name="tail",
    )(x2, y5, yb, ga, gb, *consts)


def kernel(x, norm_mix, w_in, s5_a_re, s5_a_im, s5_log_dt, s5_b_re, s5_b_im, s5_c_re, s5_c_im,
           s5_d, s5_glu_w, s5_glu_b, w_branch_a, ssd_conv_w, ssd_conv_b, ssd_dt_bias, ssd_a_log,
           ssd_d, ssd_norm, w_branch_b, w_out, norm_mlp, w_ff1, w_ff2, norm_final):
    bsz, seq, d_model = x.shape
    depth = w_in.shape[0]
    s5_width = s5_glu_w.shape[-1]
    ssd_width = w_branch_b.shape[1]
    conv_dim = ssd_conv_w.shape[-1]
    heads = ssd_a_log.shape[-1]
    widths = (s5_width, ssd_width, conv_dim, d_model, d_model, LANES)
    o_u = 0
    o_z = o_u + s5_width
    o_xbc = o_z + ssd_width
    o_dt = o_xbc + conv_dim
    o_ga = o_dt + heads
    o_gb = o_ga + d_model
    row = lambda v: v.reshape(1, -1).astype(F32)
    pad_lanes = lambda v: jnp.pad(v.reshape(1, -1).astype(F32), ((0, 0), (0, LANES - v.shape[-1])))

    x2 = x.reshape(bsz * seq, d_model)
    for i in range(depth):
        w = w_in[i]
        w_perm = jnp.concatenate(
            [w[:, o_u:o_z], w[:, o_z:o_xbc], w[:, o_xbc:o_dt], w[:, o_ga:o_gb], w[:, o_gb:],
             jnp.pad(w[:, o_dt:o_ga], ((0, 0), (0, LANES - heads)))], axis=1).astype(BF16)
        u, z, xbc, ga, gb, dt_raw = _in_proj(x2, row(norm_mix[i]), w_perm, widths)

        wi, ws, wo5, are, aim = _s5_operators(s5_a_re[i], s5_a_im[i], s5_log_dt[i], s5_b_re[i],
                                              s5_b_im[i], s5_c_re[i], s5_c_im[i])
        y5 = _s5_branch(u.reshape(bsz, seq, s5_width), wi, ws, wo5, are, aim,
                        s5_d[i].reshape(s5_width // LANES, 1, LANES))

        yb = _ssd_branch(
            xbc.reshape(bsz, seq, conv_dim), z.reshape(bsz, seq, ssd_width),
            dt_raw.reshape(bsz, seq, LANES), ssd_conv_w[i], row(ssd_conv_b[i]),
            pad_lanes(ssd_dt_bias[i]), pad_lanes(-jnp.exp(ssd_a_log[i])),
            row(jnp.repeat(ssd_d[i], SSD_HEADDIM)), row(ssd_norm[i]))

        x2 = _tail(x2, y5.reshape(bsz * seq, s5_width), yb.reshape(bsz * seq, ssd_width), ga, gb,
                   s5_glu_w[i].astype(BF16), row(s5_glu_b[i]), w_branch_a[i].astype(BF16),
                   w_branch_b[i].astype(BF16), w_out[i].astype(BF16), row(norm_mlp[i]),
                   w_ff1[i].astype(BF16), w_ff2[i].astype(BF16), row(norm_final),
                   apply_final=(i == depth - 1))
    return x2.reshape(bsz, seq, d_model)
```

```python
import functools
import math

import jax
import jax.numpy as jnp
from jax import lax
from jax.experimental import pallas as pl
from jax.experimental.pallas import tpu as pltpu

F32 = jnp.float32
BF16 = jnp.bfloat16

EPS = 1e-6
LANES = 128
SUBLANES = 8

S5_GROUP = 16
S5_STATE = 64
S5_Q = 8
S5_GROUPS_PER_TILE = LANES // S5_GROUP
S5_TT = 512

SSD_HEADDIM = 64
SSD_STATE = 128
SSD_CONV = 4
SSD_Q = 128
SSD_TT = 512
LOG2E = math.log2(math.e)

TOKEN_TILE = 512
VMEM_LIMIT = 56 * 1024 * 1024


def _rms(x, g):
    return x * lax.rsqrt(jnp.mean(x * x, axis=-1, keepdims=True) + EPS) * g


def _sigmoid(x):
    return 1.0 / (1.0 + jnp.exp(-x))


def _dot(a, b):
    return jnp.dot(a, b, preferred_element_type=F32)


def _inproj_kernel(x_ref, g_ref, w_ref, u_ref, z_ref, xbc_ref, ga_ref, gb_ref, dt_ref):
    hb = _rms(x_ref[...], g_ref[...]).astype(BF16)
    lo = 0
    for ref in (u_ref, z_ref, xbc_ref, ga_ref, gb_ref, dt_ref):
        n = ref.shape[-1]
        ref[...] = _dot(hb, w_ref[:, lo:lo + n])
        lo += n


def _in_proj(x2, g, w, widths):
    t, d = x2.shape
    tm = TOKEN_TILE
    row = lambda n: pl.BlockSpec((tm, n), lambda i: (i, 0))
    const = lambda shape: pl.BlockSpec(shape, lambda i: (0, 0), pipeline_mode=pl.Buffered(1))
    return pl.pallas_call(
        _inproj_kernel,
        grid=(t // tm,),
        in_specs=[row(d), const((1, d)), const(w.shape)],
        out_specs=[row(n) for n in widths],
        out_shape=[jax.ShapeDtypeStruct((t, n), F32) for n in widths],
        compiler_params=pltpu.CompilerParams(
            dimension_semantics=("arbitrary",), vmem_limit_bytes=VMEM_LIMIT),
        name="in_proj",
    )(x2, g, w)


def _s5_operators(a_re, a_im, log_dt, b_re, b_im, c_re, c_im):
    q = S5_Q
    g, p = a_re.shape
    hp = lax.Precision.HIGHEST
    dt = jnp.exp(log_dt)[:, None]
    steps = jnp.arange(q + 1, dtype=F32)[:, None, None]
    mag = jnp.exp(a_re * dt * steps)
    ang = a_im * dt * steps
    pw_re = mag * jnp.cos(ang)
    pw_im = mag * jnp.sin(ang)
    nr = pw_re[1] - 1.0
    ni = pw_im[1]
    den = a_re * a_re + a_im * a_im
    cr = (nr * a_re + ni * a_im) / den
    ci = (ni * a_re - nr * a_im) / den
    bb_re = cr[..., None] * b_re - ci[..., None] * b_im
    bb_im = cr[..., None] * b_im + ci[..., None] * b_re
    cp_re = c_re[None] * pw_re[:, :, None, :] - c_im[None] * pw_im[:, :, None, :]
    cp_im = c_re[None] * pw_im[:, :, None, :] + c_im[None] * pw_re[:, :, None, :]
    kj = (jnp.einsum("jghp,gpk->jghk", cp_re[:q], bb_re, precision=hp)
          - jnp.einsum("jghp,gpk->jghk", cp_im[:q], bb_im, precision=hp))
    nt = g // S5_GROUPS_PER_TILE
    g8 = S5_GROUPS_PER_TILE
    hh = S5_GROUP
    kq = q * LANES
    rep_th = jnp.kron(jnp.eye(q, dtype=F32), jnp.tile(jnp.eye(hh, dtype=F32), (1, g8))).astype(BF16)
    rep_sp = jnp.kron(jnp.eye(2, dtype=F32), jnp.tile(jnp.eye(p, dtype=F32), (1, g8))).astype(BF16)
    grp_of_chan_lane = (jnp.arange(kq) % LANES) // hh
    grp_of_state_lane = (jnp.arange(2 * g8 * p) % (g8 * p)) // p

    def expand(compact, rep, row_grp, col_grp):
        wide = jnp.dot(compact.astype(BF16), rep, preferred_element_type=F32)
        return jnp.where(row_grp[:, None] == col_grp[None, :], wide, 0.0).astype(BF16)

    kd = kj.transpose(1, 3, 0, 2).reshape(nt, LANES, q * hh)
    kdx = expand(kd, rep_th, grp_of_chan_lane[:LANES], grp_of_chan_lane)
    wi = jnp.stack([jnp.pad(kdx[:, :, :(q - j) * LANES], ((0, 0), (0, 0), (j * LANES, 0)))
                    for j in range(q)], axis=1).reshape(nt, kq, kq)
    rev_steps = (q - 1) - steps[:q]
    rev_mag = jnp.exp(a_re * dt * rev_steps)
    rev_ang = a_im * dt * rev_steps
    rev_re = (rev_mag * jnp.cos(rev_ang))[..., None]
    rev_im = (rev_mag * jnp.sin(rev_ang))[..., None]
    sb_re = rev_re * bb_re[None] - rev_im * bb_im[None]
    sb_im = rev_re * bb_im[None] + rev_im * bb_re[None]
    sb = jnp.stack([sb_re, sb_im], axis=0).reshape(2, q, nt, g8, p, hh)
    sbd = sb.transpose(2, 1, 3, 5, 0, 4).reshape(nt, kq, 2 * p)
    ws = expand(sbd, rep_sp, grp_of_chan_lane, grp_of_state_lane)
    co = jnp.stack([cp_re[1:], -cp_im[1:]], axis=0).reshape(2, q, nt, g8, hh, p)
    cod = co.transpose(2, 0, 3, 5, 1, 4).reshape(nt, 2 * g8 * p, q * hh)
    wo = expand(cod, rep_th, grp_of_state_lane, grp_of_chan_lane)
    are = pw_re[q].reshape(nt, 1, g8 * p)
    aim = pw_im[q].reshape(nt, 1, g8 * p)
    return wi, ws, wo, are, aim


def _s5_kernel(u_ref, wi_ref, ws_ref, wo_ref, are_ref, aim_ref, d_ref, y_ref,
               sre_ref, sim_ref, v_ref, sp_ref):
    q = S5_Q
    nb, tt, _ = u_ref.shape
    c = tt // q
    r = nb * c
    half = sre_ref.shape[-1]

    @pl.when(pl.program_id(1) == 0)
    def _():
        sre_ref[...] = jnp.zeros_like(sre_ref)
        sim_ref[...] = jnp.zeros_like(sim_ref)

    us = [u_ref[:, pl.ds(j, c, stride=q), :].reshape(r, LANES) for j in range(q)]
    a = jnp.concatenate([uj.astype(BF16) for uj in us], axis=1)

    v = _dot(a, ws_ref[0])
    slabs = half // LANES
    for n in range(2 * slabs):
        v_ref[n] = v[:, n * LANES:(n + 1) * LANES]

    are = jnp.broadcast_to(are_ref[0], (nb, half))
    aim = jnp.broadcast_to(aim_ref[0], (nb, half))
    s_re = sre_ref[...]
    s_im = sim_ref[...]
    for k in range(c):
        step_rows = pl.ds(k, nb, stride=c)
        v_re = jnp.concatenate([v_ref[n, step_rows, :] for n in range(slabs)], axis=1)
        v_im = jnp.concatenate([v_ref[slabs + n, step_rows, :] for n in range(slabs)], axis=1)
        for n in range(slabs):
            sp_ref[n, step_rows, :] = s_re[:, n * LANES:(n + 1) * LANES]
            sp_ref[slabs + n, step_rows, :] = s_im[:, n * LANES:(n + 1) * LANES]
        n_re = are * s_re - aim * s_im + v_re
        n_im = are * s_im + aim * s_re + v_im
        s_re, s_im = n_re, n_im
    sre_ref[...] = s_re
    sim_ref[...] = s_im

    spb = jnp.concatenate([sp_ref[n] for n in range(2 * slabs)], axis=1).astype(BF16)
    d = d_ref[0]
    mxu = 2 * LANES
    steps_per_tile = mxu // LANES
    for n in range(q * LANES // mxu):
        kk = (n + 1) * mxu
        cols = slice(n * mxu, (n + 1) * mxu)
        yn = _dot(a[:, :kk], wi_ref[0, :kk, cols]) + _dot(spb, wo_ref[0, :, cols])
        for s in range(steps_per_tile):
            j = n * steps_per_tile + s
            yj = yn[:, s * LANES:(s + 1) * LANES] + d * us[j]
            y_ref[:, pl.ds(j, c, stride=q), :] = yj.reshape(nb, c, LANES)


def _s5_branch(u3, wi, ws, wo, are, aim, d):
    nb, seq, width = u3.shape
    nt = width // LANES
    tt = S5_TT
    kq = S5_Q * LANES
    half = are.shape[-1]
    r = nb * tt // S5_Q
    tile = pl.BlockSpec((nb, tt, LANES), lambda l, t: (0, t, l))
    per_tile = lambda shape: pl.BlockSpec((1,) + shape, lambda l, t: (l, 0, 0))
    return pl.pallas_call(
        _s5_kernel,
        grid=(nt, seq // tt),
        in_specs=[tile, per_tile((kq, kq)), per_tile((kq, 2 * half)), per_tile((2 * half, kq)),
                  per_tile((1, half)), per_tile((1, half)), per_tile((1, LANES))],
        out_specs=tile,
        out_shape=jax.ShapeDtypeStruct(u3.shape, F32),
        scratch_shapes=[pltpu.VMEM((nb, half), F32), pltpu.VMEM((nb, half), F32),
                        pltpu.VMEM((2 * half // LANES, r, LANES), F32),
                        pltpu.VMEM((2 * half // LANES, r, LANES), F32)],
        compiler_params=pltpu.CompilerParams(
            dimension_semantics=("arbitrary", "arbitrary"), vmem_limit_bytes=VMEM_LIMIT),
        name="s5_branch",
    )(u3, wi, ws, wo, are, aim, d)


def _split3(x):
    hi = x.astype(BF16)
    r1 = x - hi.astype(F32)
    mid = r1.astype(BF16)
    lo = (r1 - mid.astype(F32)).astype(BF16)
    return hi, mid, lo


def _ssd_kernel(xbc_ref, z_ref, dt_ref, cw_ref, cb_ref, dtb_ref, a2_ref, dskip_ref, ng_ref,
                y_ref, pad_ref, xc_ref, dts_ref, ys_ref, st_ref):
    tt, width = z_ref.shape
    q = SSD_Q
    n_state = SSD_STATE
    slabs = pad_ref.shape[0]
    pairs = width // LANES
    groups = (slabs - pairs) // 2
    halo = SUBLANES
    taps = cw_ref.shape[0]

    @pl.when(pl.program_id(1) == 0)
    def _():
        pad_ref[:, tt:tt + halo, :] = jnp.zeros((slabs, halo, LANES), F32)
        st_ref[...] = jnp.zeros_like(st_ref)

    for n in range(slabs):
        pad_ref[n, 0:halo, :] = pad_ref[n, tt:tt + halo, :]
        pad_ref[n, halo:, :] = xbc_ref[:, n * LANES:(n + 1) * LANES]

    half = tt // 2
    for n in range(slabs):
        cols = slice(n * LANES, (n + 1) * LANES)
        for s in range(2):
            acc = cb_ref[:, cols]
            for k in range(taps):
                start = halo - (taps - 1) + s + k
                acc = acc + cw_ref[k:k + 1, cols] * pad_ref[n, pl.ds(start, half, stride=2), :]
            xc_ref[n, pl.ds(s, half, stride=2), :] = acc * _sigmoid(acc)

    v = dt_ref[...] + dtb_ref[...]
    dts_ref[...] = jnp.maximum(v, 0.0) + jnp.log1p(jnp.exp(-jnp.abs(v)))

    row_i = lax.broadcasted_iota(jnp.int32, (q, q), 0)
    col_i = lax.broadcasted_iota(jnp.int32, (q, q), 1)
    causal = row_i >= col_i
    tri = causal.astype(BF16)
    lane = lax.broadcasted_iota(jnp.int32, (q, LANES), 1)
    first_head = lane < SSD_HEADDIM
    first_mask = first_head.astype(BF16)
    second_mask = 1.0 - first_mask

    def chunk(ci, carry):
        r0 = pl.multiple_of(ci * q, q)
        rows = pl.ds(r0, q)
        dtq = dts_ref[rows, :]
        da2 = dtq * a2_ref[...]
        hi, mid, lo = _split3(da2)
        acum = _dot(tri, hi) + _dot(tri, mid) + _dot(tri, lo)
        acum_t = acum.T
        dt_t = dtq.T
        w_t = jnp.exp2(acum_t[:, q - 1:q] - acum_t) * dt_t
        ea = jnp.exp2(acum)
        for g in range(groups):
            bg = xc_ref[pairs + g, rows, :]
            cg = xc_ref[pairs + groups + g, rows, :].astype(BF16)
            cb = lax.dot_general(cg, bg.astype(BF16), (((1,), (1,)), ((), ())),
                                 preferred_element_type=F32)
            cbm = jnp.where(causal, cb, 0.0).astype(BF16)
            bgt = bg.T.astype(BF16)
            for m in (2 * g, 2 * g + 1):
                xs = xc_ref[m, rows, :]
                mats, wts, ecols = [], [], []
                for h in (2 * m, 2 * m + 1):
                    col = jnp.broadcast_to(acum[:, h:h + 1], (q, q))
                    rowv = jnp.broadcast_to(acum_t[h:h + 1, :], (q, q))
                    dtrow = jnp.broadcast_to(dt_t[h:h + 1, :], (q, q)).astype(BF16)
                    decay = jnp.exp2(jnp.minimum(col - rowv, 0.0)).astype(BF16)
                    mats.append(cbm * decay * dtrow)
                    wts.append(bgt * jnp.broadcast_to(w_t[h:h + 1, :], (n_state, q)).astype(BF16))
                    ecols.append(jnp.broadcast_to(ea[:, h:h + 1], (q, LANES)))
                mcat = jnp.concatenate(mats, axis=1)
                wcat = jnp.concatenate(wts, axis=1)
                xsb = xs.astype(BF16)
                xbd = jnp.concatenate([xsb * first_mask, xsb * second_mask], axis=0)
                y_diag = _dot(mcat, xbd)
                st_new = _dot(wcat, xbd)
                st_prev = st_ref[m]
                ecol = jnp.where(first_head, ecols[0], ecols[1])
                y_off = _dot(cg, st_prev.astype(BF16)) * ecol
                st_ref[m] = st_prev * ecol[q - 1:q, :] + st_new
                ys_ref[rows, m * LANES:(m + 1) * LANES] = (
                    y_diag + y_off + xs * dskip_ref[:, m * LANES:(m + 1) * LANES])
        return carry

    lax.fori_loop(0, tt // q, chunk, 0)

    for blk in range(tt // q):
        r0 = blk * q
        zz = z_ref[r0:r0 + q, :]
        yg = ys_ref[r0:r0 + q, :] * (zz * _sigmoid(zz))
        y_ref[r0:r0 + q, :] = _rms(yg, ng_ref[...])


def _ssd_branch(xbc3, z3, dt3, cw, cb, dtb, a2, dskip, ng):
    nb, seq, cdim = xbc3.shape
    width = z3.shape[-1]
    tt = SSD_TT
    slabs = cdim // LANES
    tile = lambda n: pl.BlockSpec((None, tt, n), lambda b, t: (b, t, 0))
    const = lambda arr: pl.BlockSpec(arr.shape, lambda b, t: (0, 0))
    return pl.pallas_call(
        _ssd_kernel,
        grid=(nb, seq // tt),
        in_specs=[tile(cdim), tile(width), tile(LANES), const(cw), const(cb), const(dtb),
                  const(a2), const(dskip), const(ng)],
        out_specs=tile(width),
        out_shape=jax.ShapeDtypeStruct(z3.shape, F32),
        scratch_shapes=[pltpu.VMEM((slabs, tt + SUBLANES, LANES), F32),
                        pltpu.VMEM((slabs, tt, LANES), F32),
                        pltpu.VMEM((tt, LANES), F32),
                        pltpu.VMEM((tt, width), F32),
                        pltpu.VMEM((width // LANES, SSD_STATE, LANES), F32)],
        compiler_params=pltpu.CompilerParams(
            dimension_semantics=("arbitrary", "arbitrary"), vmem_limit_bytes=VMEM_LIMIT),
        name="ssd_branch",
    )(xbc3, z3, dt3, cw, cb, dtb, a2, dskip, ng)


def _gelu_tanh(x):
    return 0.5 * x * (1.0 + jnp.tanh(math.sqrt(2.0 / math.pi) * (x + 0.044715 * (x * x * x))))


def _tail_kernel(x_ref, y5_ref, yb_ref, ga_ref, gb_ref, gluw_ref, glub_ref, wa_ref, wb_ref,
                 wo_ref, nm_ref, w1_ref, w2_ref, nf_ref, o_ref, *, apply_final):
    ya = _gelu_tanh(y5_ref[...])
    ya = ya * _sigmoid(_dot(ya.astype(BF16), gluw_ref[...]) + glub_ref[...])
    ya = _dot(ya.astype(BF16), wa_ref[...])
    yb = _dot(yb_ref[...].astype(BF16), wb_ref[...])
    merged = _sigmoid(ga_ref[...]) * ya + _sigmoid(gb_ref[...]) * yb
    x1 = x_ref[...] + _dot(merged.astype(BF16), wo_ref[...])
    hb = _rms(x1, nm_ref[...]).astype(BF16)
    acc = x1
    d_ff = w1_ref.shape[-1]
    ff_block = 1024
    for k in range(d_ff // ff_block):
        cols = slice(k * ff_block, (k + 1) * ff_block)
        act = jnp.maximum(_dot(hb, w1_ref[:, cols]), 0.0)
        acc = acc + _dot((act * act).astype(BF16), w2_ref[cols, :])
    if apply_final:
        acc = _rms(acc, nf_ref[...])
    o_ref[...] = acc


def _tail(x2, y5, yb, ga, gb, gluw, glub, wa, wb, wo, nm, w1, w2, nf, apply_final):
    t, d = x2.shape
    tm = TOKEN_TILE
    row = lambda n: pl.BlockSpec((tm, n), lambda i: (i, 0))
    const = lambda arr: pl.BlockSpec(arr.shape, lambda i: (0, 0), pipeline_mode=pl.Buffered(1))
    consts = (gluw, glub, wa, wb, wo, nm, w1, w2, nf)
    return pl.pallas_call(
        functools.partial(_tail_kernel, apply_final=apply_final),
        grid=(t // tm,),
        in_specs=[row(d), row(y5.shape[-1]), row(yb.shape[-1]), row(d), row(d)]
        + [const(c) for c in consts],
        out_specs=row(d),
        out_shape=jax.ShapeDtypeStruct((t, d), F32),
        compiler_params=pltpu.CompilerParams(
            dimension_semantics=("arbitrary",), vmem_limit_bytes=VMEM_LIMIT),
        name="tail",
    )(x2, y5, yb, ga, gb, *consts)


def kernel(x, norm_mix, w_in, s5_a_re, s5_a_im, s5_log_dt, s5_b_re, s5_b_im, s5_c_re, s5_c_im,
           s5_d, s5_glu_w, s5_glu_b, w_branch_a, ssd_conv_w, ssd_conv_b, ssd_dt_bias, ssd_a_log,
           ssd_d, ssd_norm, w_branch_b, w_out, norm_mlp, w_ff1, w_ff2, norm_final):
    bsz, seq, d_model = x.shape
    depth = w_in.shape[0]
    s5_width = s5_glu_w.shape[-1]
    ssd_width = w_branch_b.shape[1]
    conv_dim = ssd_conv_w.shape[-1]
    heads = ssd_a_log.shape[-1]
    widths = (s5_width, ssd_width, conv_dim, d_model, d_model, LANES)
    o_u = 0
    o_z = o_u + s5_width
    o_xbc = o_z + ssd_width
    o_dt = o_xbc + conv_dim
    o_ga = o_dt + heads
    o_gb = o_ga + d_model
    row = lambda v: v.reshape(1, -1).astype(F32)
    pad_lanes = lambda v: jnp.pad(v.reshape(1, -1).astype(F32), ((0, 0), (0, LANES - v.shape[-1])))

    x2 = x.reshape(bsz * seq, d_model)
    for i in range(depth):
        w = w_in[i]
        w_perm = jnp.concatenate(
            [w[:, o_u:o_z], w[:, o_z:o_xbc], w[:, o_xbc:o_dt], w[:, o_ga:o_gb], w[:, o_gb:],
             jnp.pad(w[:, o_dt:o_ga], ((0, 0), (0, LANES - heads)))], axis=1).astype(BF16)
        u, z, xbc, ga, gb, dt_raw = _in_proj(x2, row(norm_mix[i]), w_perm, widths)

        wi, ws, wo5, are, aim = _s5_operators(s5_a_re[i], s5_a_im[i], s5_log_dt[i], s5_b_re[i],
                                              s5_b_im[i], s5_c_re[i], s5_c_im[i])
        y5 = _s5_branch(u.reshape(bsz, seq, s5_width), wi, ws, wo5, are, aim,
                        s5_d[i].reshape(s5_width // LANES, 1, LANES))

        yb = _ssd_branch(
            xbc.reshape(bsz, seq, conv_dim), z.reshape(bsz, seq, ssd_width),
            dt_raw.reshape(bsz, seq, LANES), ssd_conv_w[i], row(ssd_conv_b[i]),
            pad_lanes(ssd_dt_bias[i]), pad_lanes(-jnp.exp(ssd_a_log[i]) * LOG2E),
            row(jnp.repeat(ssd_d[i], SSD_HEADDIM)), row(ssd_norm[i]))

        x2 = _tail(x2, y5.reshape(bsz * seq, s5_width), yb.reshape(bsz * seq, ssd_width), ga, gb,
                   s5_glu_w[i].astype(BF16), row(s5_glu_b[i]), w_branch_a[i].astype(BF16),
                   w_branch_b[i].astype(BF16), w_out[i].astype(BF16), row(norm_mlp[i]),
                   w_ff1[i].astype(BF16), w_ff2[i].astype(BF16), row(norm_final),
                   apply_final=(i == depth - 1))
    return x2.reshape(bsz, seq, d_model)
```

```python
import functools
import math

import jax
import jax.numpy as jnp
from jax import lax
from jax.experimental import pallas as pl
from jax.experimental.pallas import tpu as pltpu

F32 = jnp.float32
BF16 = jnp.bfloat16

EPS = 1e-6
LANES = 128
SUBLANES = 8

S5_GROUP = 16
S5_STATE = 64
S5_Q = 8
S5_GROUPS_PER_TILE = LANES // S5_GROUP
S5_TT = 512
S5_ROW_PAD = SUBLANES

SSD_HEADDIM = 64
SSD_STATE = 128
SSD_CONV = 4
SSD_Q = 128
SSD_TT = 512
LOG2E = math.log2(math.e)

TOKEN_TILE = 512
VMEM_LIMIT = 56 * 1024 * 1024


def _rms(x, g):
    return x * lax.rsqrt(jnp.mean(x * x, axis=-1, keepdims=True) + EPS) * g


def _sigmoid(x):
    return 1.0 / (1.0 + jnp.exp(-x))


def _dot(a, b):
    return jnp.dot(a, b, preferred_element_type=F32)


def _inproj_kernel(x_ref, g_ref, w_ref, u_ref, z_ref, xbc_ref, ga_ref, gb_ref, dt_ref):
    hb = _rms(x_ref[...], g_ref[...]).astype(BF16)
    lo = 0
    for ref in (u_ref, z_ref, xbc_ref, ga_ref, gb_ref, dt_ref):
        n = ref.shape[-1]
        ref[...] = _dot(hb, w_ref[:, lo:lo + n])
        lo += n


def _in_proj(x2, g, w, widths):
    t, d = x2.shape
    tm = TOKEN_TILE
    row = lambda n: pl.BlockSpec((tm, n), lambda i: (i, 0))
    const = lambda shape: pl.BlockSpec(shape, lambda i: (0, 0), pipeline_mode=pl.Buffered(1))
    return pl.pallas_call(
        _inproj_kernel,
        grid=(t // tm,),
        in_specs=[row(d), const((1, d)), const(w.shape)],
        out_specs=[row(n) for n in widths],
        out_shape=[jax.ShapeDtypeStruct((t, n), F32) for n in widths],
        compiler_params=pltpu.CompilerParams(
            dimension_semantics=("arbitrary",), vmem_limit_bytes=VMEM_LIMIT),
        name="in_proj",
    )(x2, g, w)


def _s5_operators(a_re, a_im, log_dt, b_re, b_im, c_re, c_im):
    q = S5_Q
    g, p = a_re.shape
    hp = lax.Precision.HIGHEST
    dt = jnp.exp(log_dt)[:, None]
    steps = jnp.arange(q + 1, dtype=F32)[:, None, None]
    mag = jnp.exp(a_re * dt * steps)
    ang = a_im * dt * steps
    pw_re = mag * jnp.cos(ang)
    pw_im = mag * jnp.sin(ang)
    nr = pw_re[1] - 1.0
    ni = pw_im[1]
    den = a_re * a_re + a_im * a_im
    cr = (nr * a_re + ni * a_im) / den
    ci = (ni * a_re - nr * a_im) / den
    bb_re = cr[..., None] * b_re - ci[..., None] * b_im
    bb_im = cr[..., None] * b_im + ci[..., None] * b_re
    cp_re = c_re[None] * pw_re[:, :, None, :] - c_im[None] * pw_im[:, :, None, :]
    cp_im = c_re[None] * pw_im[:, :, None, :] + c_im[None] * pw_re[:, :, None, :]
    kj = (jnp.einsum("jghp,gpk->jghk", cp_re[:q], bb_re, precision=hp)
          - jnp.einsum("jghp,gpk->jghk", cp_im[:q], bb_im, precision=hp))
    nt = g // S5_GROUPS_PER_TILE
    g8 = S5_GROUPS_PER_TILE
    hh = S5_GROUP
    kq = q * LANES
    kd = kj.transpose(1, 3, 0, 2).reshape(nt, LANES, q * hh)
    rev_steps = (q - 1) - steps[:q]
    rev_mag = jnp.exp(a_re * dt * rev_steps)
    rev_ang = a_im * dt * rev_steps
    rev_re = (rev_mag * jnp.cos(rev_ang))[..., None]
    rev_im = (rev_mag * jnp.sin(rev_ang))[..., None]
    sb_re = rev_re * bb_re[None] - rev_im * bb_im[None]
    sb_im = rev_re * bb_im[None] + rev_im * bb_re[None]
    sb = jnp.stack([sb_re, sb_im], axis=0).reshape(2, q, nt, g8, p, hh)
    sbd = sb.transpose(2, 1, 3, 5, 0, 4).reshape(nt, kq, 2 * p)
    co = jnp.stack([cp_re[1:], -cp_im[1:]], axis=0).reshape(2, q, nt, g8, hh, p)
    cod = co.transpose(2, 0, 3, 5, 1, 4).reshape(nt, 2 * g8 * p, q * hh)
    are = pw_re[q].reshape(nt, 1, g8 * p)
    aim = pw_im[q].reshape(nt, 1, g8 * p)
    return kd.astype(BF16), sbd.astype(BF16), cod.astype(BF16), are, aim


def _field(idx, period, width):
    return (idx & (period - 1)) >> (width.bit_length() - 1)


def _s5_expand(compact, rep, row_grp, col_grp):
    wide = _dot(compact, rep)
    return jnp.where(row_grp == col_grp, wide, 0.0).astype(BF16)


def _s5_build_operators(kd_ref, sbd_ref, cod_ref, wi_ref, ws_ref, wo_ref):
    q = S5_Q
    hh = S5_GROUP
    kq = q * LANES
    p = sbd_ref.shape[-1] // 2
    ns = wo_ref.shape[0]
    iota = lambda shape, d: lax.broadcasted_iota(jnp.int32, shape, d)
    r1, c1 = iota((q * hh, kq), 0), iota((q * hh, kq), 1)
    rep_th = ((_field(r1, q * hh, hh) == _field(c1, kq, LANES))
              & (_field(r1, hh, 1) == _field(c1, hh, 1))).astype(BF16)
    r2, c2 = iota((2 * p, ns), 0), iota((2 * p, ns), 1)
    rep_sp = ((_field(r2, 2 * p, p) == _field(c2, ns, ns // 2))
              & (_field(r2, p, 1) == _field(c2, p, 1))).astype(BF16)
    chan_grp = lambda idx: _field(idx, LANES, hh)
    state_grp = lambda idx: _field(idx, ns // 2, p)
    kdx = _s5_expand(kd_ref[0], rep_th, chan_grp(iota((LANES, kq), 0)), chan_grp(iota((LANES, kq), 1)))
    wi_ref[...] = jnp.zeros_like(wi_ref)
    for j in range(q):
        wi_ref[j * LANES:(j + 1) * LANES, j * LANES:] = kdx[:, :(q - j) * LANES]
    ws_ref[...] = _s5_expand(sbd_ref[0], rep_sp, chan_grp(iota((kq, ns), 0)), state_grp(iota((kq, ns), 1)))
    wo_ref[...] = _s5_expand(cod_ref[0], rep_th, state_grp(iota((ns, kq), 0)), chan_grp(iota((ns, kq), 1)))


def _s5_kernel(u_ref, kd_ref, sbd_ref, cod_ref, are_ref, aim_ref, d_ref, y_ref,
               sre_ref, sim_ref, v_ref, sp_ref, wi_ref, ws_ref, wo_ref):
    q = S5_Q
    nb, tt, _ = u_ref.shape
    c = tt // q
    r = nb * c
    half = sre_ref.shape[-1]

    @pl.when(pl.program_id(1) == 0)
    def _():
        sre_ref[...] = jnp.zeros_like(sre_ref)
        sim_ref[...] = jnp.zeros_like(sim_ref)
        _s5_build_operators(kd_ref, sbd_ref, cod_ref, wi_ref, ws_ref, wo_ref)

    us = [u_ref[:, pl.ds(j, c, stride=q), :].reshape(r, LANES) for j in range(q)]
    a = jnp.concatenate([uj.astype(BF16) for uj in us], axis=1)

    v = _dot(a, ws_ref[...])
    slabs = half // LANES
    pitch = v_ref.shape[1] // nb
    for n in range(2 * slabs):
        for b in range(nb):
            v_ref[n, b * pitch:b * pitch + c, :] = v[b * c:(b + 1) * c, n * LANES:(n + 1) * LANES]

    are = jnp.broadcast_to(are_ref[0], (nb, half))
    aim = jnp.broadcast_to(aim_ref[0], (nb, half))
    s_re = sre_ref[...]
    s_im = sim_ref[...]
    for k in range(c):
        step_rows = pl.ds(k, nb, stride=pitch)
        v_re = jnp.concatenate([v_ref[n, step_rows, :] for n in range(slabs)], axis=1)
        v_im = jnp.concatenate([v_ref[slabs + n, step_rows, :] for n in range(slabs)], axis=1)
        for n in range(slabs):
            sp_ref[n, step_rows, :] = s_re[:, n * LANES:(n + 1) * LANES]
            sp_ref[slabs + n, step_rows, :] = s_im[:, n * LANES:(n + 1) * LANES]
        n_re = are * s_re - aim * s_im + v_re
        n_im = are * s_im + aim * s_re + v_im
        s_re, s_im = n_re, n_im
    sre_ref[...] = s_re
    sim_ref[...] = s_im

    spb = jnp.concatenate(
        [jnp.concatenate([sp_ref[n, b * pitch:b * pitch + c, :] for b in range(nb)], axis=0)
         for n in range(2 * slabs)], axis=1).astype(BF16)
    d = d_ref[0]
    mxu = 2 * LANES
    steps_per_tile = mxu // LANES
    for n in range(q * LANES // mxu):
        kk = (n + 1) * mxu
        cols = slice(n * mxu, (n + 1) * mxu)
        yn = _dot(a[:, :kk], wi_ref[:kk, cols]) + _dot(spb, wo_ref[:, cols])
        for s in range(steps_per_tile):
            j = n * steps_per_tile + s
            yj = yn[:, s * LANES:(s + 1) * LANES] + d * us[j]
            y_ref[:, pl.ds(j, c, stride=q), :] = yj.reshape(nb, c, LANES)


def _s5_branch(u3, kd, sbd, cod, are, aim, d):
    nb, seq, width = u3.shape
    nt = width // LANES
    tt = S5_TT
    kq = S5_Q * LANES
    half = are.shape[-1]
    r = nb * (tt // S5_Q + S5_ROW_PAD)
    tile = pl.BlockSpec((nb, tt, LANES), lambda l, t: (0, t, l))
    per_tile = lambda arr: pl.BlockSpec((1,) + arr.shape[1:], lambda l, t: (l, 0, 0))
    return pl.pallas_call(
        _s5_kernel,
        grid=(nt, seq // tt),
        in_specs=[tile] + [per_tile(arr) for arr in (kd, sbd, cod, are, aim, d)],
        out_specs=tile,
        out_shape=jax.ShapeDtypeStruct(u3.shape, F32),
        scratch_shapes=[pltpu.VMEM((nb, half), F32), pltpu.VMEM((nb, half), F32),
                        pltpu.VMEM((2 * half // LANES, r, LANES), F32),
                        pltpu.VMEM((2 * half // LANES, r, LANES), F32),
                        pltpu.VMEM((kq, kq), BF16), pltpu.VMEM((kq, 2 * half), BF16),
                        pltpu.VMEM((2 * half, kq), BF16)],
        compiler_params=pltpu.CompilerParams(
            dimension_semantics=("arbitrary", "arbitrary"), vmem_limit_bytes=VMEM_LIMIT),
        name="s5_branch",
    )(u3, kd, sbd, cod, are, aim, d)


def _split3(x):
    hi = x.astype(BF16)
    r1 = x - hi.astype(F32)
    mid = r1.astype(BF16)
    lo = (r1 - mid.astype(F32)).astype(BF16)
    return hi, mid, lo


def _ssd_kernel(xbc_ref, z_ref, dt_ref, cw_ref, cb_ref, dtb_ref, a2_ref, dskip_ref, ng_ref,
                y_ref, pad_ref, xc_ref, acum_ref, acumt_ref, dtt_ref, wt_ref, ys_ref, st_ref):
    tt, width = z_ref.shape
    q = SSD_Q
    n_state = SSD_STATE
    slabs = pad_ref.shape[0]
    pairs = width // LANES
    groups = (slabs - pairs) // 2
    halo = SUBLANES
    taps = cw_ref.shape[0]

    @pl.when(pl.program_id(1) == 0)
    def _():
        pad_ref[:, tt:tt + halo, :] = jnp.zeros((slabs, halo, LANES), F32)
        st_ref[...] = jnp.zeros_like(st_ref)

    for n in range(slabs):
        pad_ref[n, 0:halo, :] = pad_ref[n, tt:tt + halo, :]
        pad_ref[n, halo:, :] = xbc_ref[:, n * LANES:(n + 1) * LANES]

    half = tt // 2
    for n in range(slabs):
        cols = slice(n * LANES, (n + 1) * LANES)
        for s in range(2):
            acc = cb_ref[:, cols]
            for k in range(taps):
                start = halo - (taps - 1) + s + k
                acc = acc + cw_ref[k:k + 1, cols] * pad_ref[n, pl.ds(start, half, stride=2), :]
            xc_ref[n, pl.ds(s, half, stride=2), :] = acc + acc * jnp.tanh(acc)

    row_i = lax.broadcasted_iota(jnp.int32, (q, q), 0)
    col_i = lax.broadcasted_iota(jnp.int32, (q, q), 1)
    causal = row_i >= col_i
    tri = causal.astype(BF16)
    lane = lax.broadcasted_iota(jnp.int32, (q, LANES), 1)
    first_head = lane < SSD_HEADDIM
    first_mask = first_head.astype(BF16)
    second_mask = 1.0 - first_mask

    v = dt_ref[...] + dtb_ref[...]
    dts = jnp.maximum(v, 0.0) + jnp.log1p(jnp.exp(-jnp.abs(v)))
    for ci in range(tt // q):
        rows = slice(ci * q, (ci + 1) * q)
        dtq = dts[rows]
        hi, mid, lo = _split3(dtq * a2_ref[...])
        acum = _dot(tri, hi) + _dot(tri, mid) + _dot(tri, lo)
        acum_t = acum.T
        dt_t = dtq.T
        acum_ref[rows, :] = acum
        acumt_ref[ci] = acum_t
        dtt_ref[ci] = dt_t
        wt_ref[ci] = jnp.exp2(acum_t[:, q - 1:q] - acum_t) * dt_t

    for ci in range(tt // q):
        rows = slice(ci * q, (ci + 1) * q)
        acum = acum_ref[rows, :]
        acum_t = acumt_ref[ci]
        dt_t = dtt_ref[ci]
        w_t = wt_ref[ci]
        for g in range(groups):
            bg = xc_ref[pairs + g, rows, :]
            cg = xc_ref[pairs + groups + g, rows, :].astype(BF16)
            cb = lax.dot_general(cg, bg.astype(BF16), (((1,), (1,)), ((), ())),
                                 preferred_element_type=F32)
            cbm = jnp.where(causal, cb, 0.0).astype(BF16)
            bgt = bg.T.astype(BF16)
            for m in (2 * g, 2 * g + 1):
                xs = xc_ref[m, rows, :]
                mats, wts, cols = [], [], []
                for h in (2 * m, 2 * m + 1):
                    col = jnp.broadcast_to(acum[:, h:h + 1], (q, q))
                    cols.append(col)
                    rowv = jnp.broadcast_to(acum_t[h:h + 1, :], (q, q))
                    dtrow = jnp.broadcast_to(dt_t[h:h + 1, :], (q, q)).astype(BF16)
                    decay = jnp.exp2(jnp.minimum(col - rowv, 0.0)).astype(BF16)
                    mats.append(cbm * decay * dtrow)
                    wts.append(bgt * jnp.broadcast_to(w_t[h:h + 1, :], (n_state, q)).astype(BF16))
                mcat = jnp.concatenate(mats, axis=1)
                wcat = jnp.concatenate(wts, axis=1)
                xsb = xs.astype(BF16)
                xbd = jnp.concatenate([xsb * first_mask, xsb * second_mask], axis=0)
                y_diag = _dot(mcat, xbd)
                st_new = _dot(wcat, xbd)
                st_prev = st_ref[m]
                ecol = jnp.exp2(jnp.where(first_head, cols[0], cols[1]))
                y_off = _dot(cg, st_prev.astype(BF16)) * ecol
                st_ref[m] = st_prev * ecol[q - 1:q, :] + st_new
                ys_ref[rows, m * LANES:(m + 1) * LANES] = (
                    y_diag + y_off + xs * dskip_ref[:, m * LANES:(m + 1) * LANES])
        hz = 0.5 * z_ref[rows, :]
        yg = ys_ref[rows, :] * (hz + hz * jnp.tanh(hz))
        y_ref[rows, :] = _rms(yg, ng_ref[...])


def _ssd_branch(xbc3, z3, dt3, cw, cb, dtb, a2, dskip, ng):
    nb, seq, cdim = xbc3.shape
    width = z3.shape[-1]
    tt = SSD_TT
    slabs = cdim // LANES
    tile = lambda n: pl.BlockSpec((None, tt, n), lambda b, t: (b, t, 0))
    const = lambda arr: pl.BlockSpec(arr.shape, lambda b, t: (0, 0))
    return pl.pallas_call(
        _ssd_kernel,
        grid=(nb, seq // tt),
        in_specs=[tile(cdim), tile(width), tile(LANES), const(cw), const(cb), const(dtb),
                  const(a2), const(dskip), const(ng)],
        out_specs=tile(width),
        out_shape=jax.ShapeDtypeStruct(z3.shape, F32),
        scratch_shapes=[pltpu.VMEM((slabs, tt + SUBLANES, LANES), F32),
                        pltpu.VMEM((slabs, tt, LANES), F32),
                        pltpu.VMEM((tt, LANES), F32),
                        pltpu.VMEM((tt // SSD_Q, SSD_Q, SSD_Q), F32),
                        pltpu.VMEM((tt // SSD_Q, SSD_Q, SSD_Q), F32),
                        pltpu.VMEM((tt // SSD_Q, SSD_Q, SSD_Q), F32),
                        pltpu.VMEM((tt, width), F32),
                        pltpu.VMEM((width // LANES, SSD_STATE, LANES), F32)],
        compiler_params=pltpu.CompilerParams(
            dimension_semantics=("arbitrary", "arbitrary"), vmem_limit_bytes=VMEM_LIMIT),
        name="ssd_branch",
    )(xbc3, z3, dt3, cw, cb, dtb, a2, dskip, ng)


def _gelu_tanh(x):
    return 0.5 * x * (1.0 + jnp.tanh(math.sqrt(2.0 / math.pi) * (x + 0.044715 * (x * x * x))))


def _tail_kernel(x_ref, y5_ref, yb_ref, ga_ref, gb_ref, gluw_ref, glub_ref, wa_ref, wb_ref,
                 wo_ref, nm_ref, w1_ref, w2_ref, nf_ref, o_ref, *, apply_final):
    ya = _gelu_tanh(y5_ref[...])
    ya = ya * _sigmoid(_dot(ya.astype(BF16), gluw_ref[...]) + glub_ref[...])
    ya = _dot(ya.astype(BF16), wa_ref[...])
    yb = _dot(yb_ref[...].astype(BF16), wb_ref[...])
    merged = _sigmoid(ga_ref[...]) * ya + _sigmoid(gb_ref[...]) * yb
    x1 = x_ref[...] + _dot(merged.astype(BF16), wo_ref[...])
    hb = _rms(x1, nm_ref[...]).astype(BF16)
    acc = x1
    d_ff = w1_ref.shape[-1]
    ff_block = 1024
    for k in range(d_ff // ff_block):
        cols = slice(k * ff_block, (k + 1) * ff_block)
        act = jnp.maximum(_dot(hb, w1_ref[:, cols]), 0.0)
        acc = acc + _dot((act * act).astype(BF16), w2_ref[cols, :])
    if apply_final:
        acc = _rms(acc, nf_ref[...])
    o_ref[...] = acc


def _tail(x2, y5, yb, ga, gb, gluw, glub, wa, wb, wo, nm, w1, w2, nf, apply_final):
    t, d = x2.shape
    tm = TOKEN_TILE
    row = lambda n: pl.BlockSpec((tm, n), lambda i: (i, 0))
    const = lambda arr: pl.BlockSpec(arr.shape, lambda i: (0, 0), pipeline_mode=pl.Buffered(1))
    consts = (gluw, glub, wa, wb, wo, nm, w1, w2, nf)
    return pl.pallas_call(
        functools.partial(_tail_kernel, apply_final=apply_final),
        grid=(t // tm,),
        in_specs=[row(d), row(y5.shape[-1]), row(yb.shape[-1]), row(d), row(d)]
        + [const(c) for c in consts],
        out_specs=row(d),
        out_shape=jax.ShapeDtypeStruct((t, d), F32),
        compiler_params=pltpu.CompilerParams(
            dimension_semantics=("arbitrary",), vmem_limit_bytes=VMEM_LIMIT),
        name="tail",
    )(x2, y5, yb, ga, gb, *consts)


def kernel(x, norm_mix, w_in, s5_a_re, s5_a_im, s5_log_dt, s5_b_re, s5_b_im, s5_c_re, s5_c_im,
           s5_d, s5_glu_w, s5_glu_b, w_branch_a, ssd_conv_w, ssd_conv_b, ssd_dt_bias, ssd_a_log,
           ssd_d, ssd_norm, w_branch_b, w_out, norm_mlp, w_ff1, w_ff2, norm_final):
    bsz, seq, d_model = x.shape
    depth = w_in.shape[0]
    s5_width = s5_glu_w.shape[-1]
    ssd_width = w_branch_b.shape[1]
    conv_dim = ssd_conv_w.shape[-1]
    heads = ssd_a_log.shape[-1]
    widths = (s5_width, ssd_width, conv_dim, d_model, d_model, LANES)
    o_u = 0
    o_z = o_u + s5_width
    o_xbc = o_z + ssd_width
    o_dt = o_xbc + conv_dim
    o_ga = o_dt + heads
    o_gb = o_ga + d_model
    row = lambda v: v.reshape(1, -1).astype(F32)
    pad_lanes = lambda v: jnp.pad(v.reshape(1, -1).astype(F32), ((0, 0), (0, LANES - v.shape[-1])))

    x2 = x.reshape(bsz * seq, d_model)
    for i in range(depth):
        w = w_in[i]
        w_perm = jnp.concatenate(
            [w[:, o_u:o_z], w[:, o_z:o_xbc], w[:, o_xbc:o_dt], w[:, o_ga:o_gb], w[:, o_gb:],
             jnp.pad(w[:, o_dt:o_ga], ((0, 0), (0, LANES - heads)))], axis=1).astype(BF16)
        u, z, xbc, ga, gb, dt_raw = _in_proj(x2, row(norm_mix[i]), w_perm, widths)

        kd, sbd, cod, are, aim = _s5_operators(s5_a_re[i], s5_a_im[i], s5_log_dt[i], s5_b_re[i],
                                               s5_b_im[i], s5_c_re[i], s5_c_im[i])
        y5 = _s5_branch(u.reshape(bsz, seq, s5_width), kd, sbd, cod, are, aim,
                        s5_d[i].reshape(s5_width // LANES, 1, LANES))

        yb = _ssd_branch(
            xbc.reshape(bsz, seq, conv_dim), z.reshape(bsz, seq, ssd_width),
            dt_raw.reshape(bsz, seq, LANES), 0.5 * ssd_conv_w[i], row(0.5 * ssd_conv_b[i]),
            pad_lanes(ssd_dt_bias[i]), pad_lanes(-jnp.exp(ssd_a_log[i]) * LOG2E),
            row(jnp.repeat(ssd_d[i], SSD_HEADDIM)), row(ssd_norm[i]))

        x2 = _tail(x2, y5.reshape(bsz * seq, s5_width), yb.reshape(bsz * seq, ssd_width), ga, gb,
                   s5_glu_w[i].astype(BF16), row(s5_glu_b[i]), w_branch_a[i].astype(BF16),
                   w_branch_b[i].astype(BF16), w_out[i].astype(BF16), row(norm_mlp[i]),
                   w_ff1[i].astype(BF16), w_ff2[i].astype(BF16), row(norm_final),
                   apply_final=(i == depth - 1))
    return x2.reshape(bsz, seq, d_model)
```

```python
import functools
import math

import jax
import jax.numpy as jnp
from jax import lax
from jax.experimental import pallas as pl
from jax.experimental.pallas import tpu as pltpu

F32 = jnp.float32
BF16 = jnp.bfloat16

EPS = 1e-6
LANES = 128
SUBLANES = 8

S5_GROUP = 16
S5_STATE = 64
S5_Q = 8
S5_GROUPS_PER_TILE = LANES // S5_GROUP
S5_TT = 512
S5_ROW_PAD = SUBLANES

SSD_HEADDIM = 64
SSD_STATE = 128
SSD_CONV = 4
SSD_Q = 128
SSD_TT = 512
LOG2E = math.log2(math.e)

TOKEN_TILE = 512
VMEM_LIMIT = 56 * 1024 * 1024


def _rms(x, g):
    return x * lax.rsqrt(jnp.mean(x * x, axis=-1, keepdims=True) + EPS) * g


def _sigmoid(x):
    return 1.0 / (1.0 + jnp.exp(-x))


def _dot(a, b):
    return jnp.dot(a, b, preferred_element_type=F32)


def _s5_operators(a_re, a_im, log_dt, b_re, b_im, c_re, c_im):
    q = S5_Q
    g, p = a_re.shape
    hp = lax.Precision.HIGHEST
    dt = jnp.exp(log_dt)[:, None]
    steps = jnp.arange(q + 1, dtype=F32)[:, None, None]
    mag = jnp.exp(a_re * dt * steps)
    ang = a_im * dt * steps
    pw_re = mag * jnp.cos(ang)
    pw_im = mag * jnp.sin(ang)
    nr = pw_re[1] - 1.0
    ni = pw_im[1]
    den = a_re * a_re + a_im * a_im
    cr = (nr * a_re + ni * a_im) / den
    ci = (ni * a_re - nr * a_im) / den
    bb_re = cr[..., None] * b_re - ci[..., None] * b_im
    bb_im = cr[..., None] * b_im + ci[..., None] * b_re
    cp_re = c_re[None] * pw_re[:, :, None, :] - c_im[None] * pw_im[:, :, None, :]
    cp_im = c_re[None] * pw_im[:, :, None, :] + c_im[None] * pw_re[:, :, None, :]
    kj = (jnp.einsum("jghp,gpk->jghk", cp_re[:q], bb_re, precision=hp)
          - jnp.einsum("jghp,gpk->jghk", cp_im[:q], bb_im, precision=hp))
    nt = g // S5_GROUPS_PER_TILE
    g8 = S5_GROUPS_PER_TILE
    hh = S5_GROUP
    kq = q * LANES
    kd = kj.transpose(1, 3, 0, 2).reshape(nt, LANES, q * hh)
    rev_steps = (q - 1) - steps[:q]
    rev_mag = jnp.exp(a_re * dt * rev_steps)
    rev_ang = a_im * dt * rev_steps
    rev_re = (rev_mag * jnp.cos(rev_ang))[..., None]
    rev_im = (rev_mag * jnp.sin(rev_ang))[..., None]
    sb_re = rev_re * bb_re[None] - rev_im * bb_im[None]
    sb_im = rev_re * bb_im[None] + rev_im * bb_re[None]
    sb = jnp.stack([sb_re, sb_im], axis=0).reshape(2, q, nt, g8, p, hh)
    sbd = sb.transpose(2, 1, 3, 5, 0, 4).reshape(nt, kq, 2 * p)
    co = jnp.stack([cp_re[1:], -cp_im[1:]], axis=0).reshape(2, q, nt, g8, hh, p)
    cod = co.transpose(2, 0, 3, 5, 1, 4).reshape(nt, 2 * g8 * p, q * hh)
    are = pw_re[q].reshape(nt, 1, g8 * p)
    aim = pw_im[q].reshape(nt, 1, g8 * p)
    return kd.astype(BF16), sbd.astype(BF16), cod.astype(BF16), are, aim


def _field(idx, period, width):
    return (idx & (period - 1)) >> (width.bit_length() - 1)


def _s5_expand(compact, rep, row_grp, col_grp):
    wide = _dot(compact, rep)
    return jnp.where(row_grp == col_grp, wide, 0.0).astype(BF16)


def _s5_build_operators(kd_ref, sbd_ref, cod_ref, wi_ref, ws_ref, wo_ref):
    q = S5_Q
    hh = S5_GROUP
    kq = q * LANES
    p = sbd_ref.shape[-1] // 2
    ns = wo_ref.shape[0]
    iota = lambda shape, d: lax.broadcasted_iota(jnp.int32, shape, d)
    r1, c1 = iota((q * hh, kq), 0), iota((q * hh, kq), 1)
    rep_th = ((_field(r1, q * hh, hh) == _field(c1, kq, LANES))
              & (_field(r1, hh, 1) == _field(c1, hh, 1))).astype(BF16)
    r2, c2 = iota((2 * p, ns), 0), iota((2 * p, ns), 1)
    rep_sp = ((_field(r2, 2 * p, p) == _field(c2, ns, ns // 2))
              & (_field(r2, p, 1) == _field(c2, p, 1))).astype(BF16)
    chan_grp = lambda idx: _field(idx, LANES, hh)
    state_grp = lambda idx: _field(idx, ns // 2, p)
    kdx = _s5_expand(kd_ref[0], rep_th, chan_grp(iota((LANES, kq), 0)), chan_grp(iota((LANES, kq), 1)))
    wi_ref[...] = jnp.zeros_like(wi_ref)
    for j in range(q):
        wi_ref[j * LANES:(j + 1) * LANES, j * LANES:] = kdx[:, :(q - j) * LANES]
    ws_ref[...] = _s5_expand(sbd_ref[0], rep_sp, chan_grp(iota((kq, ns), 0)), state_grp(iota((kq, ns), 1)))
    wo_ref[...] = _s5_expand(cod_ref[0], rep_th, state_grp(iota((ns, kq), 0)), chan_grp(iota((ns, kq), 1)))


def _s5_kernel(u_ref, kd_ref, sbd_ref, cod_ref, are_ref, aim_ref, d_ref, y_ref,
               sre_ref, sim_ref, v_ref, sp_ref, wi_ref, ws_ref, wo_ref):
    q = S5_Q
    nb, tt, _ = u_ref.shape
    c = tt // q
    r = nb * c
    half = sre_ref.shape[-1]

    @pl.when(pl.program_id(1) == 0)
    def _():
        sre_ref[...] = jnp.zeros_like(sre_ref)
        sim_ref[...] = jnp.zeros_like(sim_ref)
        _s5_build_operators(kd_ref, sbd_ref, cod_ref, wi_ref, ws_ref, wo_ref)

    us = [u_ref[:, pl.ds(j, c, stride=q), :].reshape(r, LANES) for j in range(q)]
    a = jnp.concatenate([uj.astype(BF16) for uj in us], axis=1)

    v = _dot(a, ws_ref[...])
    slabs = half // LANES
    pitch = v_ref.shape[1] // nb
    for n in range(2 * slabs):
        for b in range(nb):
            v_ref[n, b * pitch:b * pitch + c, :] = v[b * c:(b + 1) * c, n * LANES:(n + 1) * LANES]

    are = jnp.broadcast_to(are_ref[0], (nb, half))
    aim = jnp.broadcast_to(aim_ref[0], (nb, half))
    s_re = sre_ref[...]
    s_im = sim_ref[...]
    for k in range(c):
        step_rows = pl.ds(k, nb, stride=pitch)
        v_re = jnp.concatenate([v_ref[n, step_rows, :] for n in range(slabs)], axis=1)
        v_im = jnp.concatenate([v_ref[slabs + n, step_rows, :] for n in range(slabs)], axis=1)
        for n in range(slabs):
            sp_ref[n, step_rows, :] = s_re[:, n * LANES:(n + 1) * LANES]
            sp_ref[slabs + n, step_rows, :] = s_im[:, n * LANES:(n + 1) * LANES]
        n_re = are * s_re - aim * s_im + v_re
        n_im = are * s_im + aim * s_re + v_im
        s_re, s_im = n_re, n_im
    sre_ref[...] = s_re
    sim_ref[...] = s_im

    spb = jnp.concatenate(
        [jnp.concatenate([sp_ref[n, b * pitch:b * pitch + c, :] for b in range(nb)], axis=0)
         for n in range(2 * slabs)], axis=1).astype(BF16)
    d = d_ref[0]
    mxu = 2 * LANES
    steps_per_tile = mxu // LANES
    for n in range(q * LANES // mxu):
        kk = (n + 1) * mxu
        cols = slice(n * mxu, (n + 1) * mxu)
        yn = _dot(a[:, :kk], wi_ref[:kk, cols]) + _dot(spb, wo_ref[:, cols])
        for s in range(steps_per_tile):
            j = n * steps_per_tile + s
            yj = yn[:, s * LANES:(s + 1) * LANES] + d * us[j]
            y_ref[:, pl.ds(j, c, stride=q), :] = yj.reshape(nb, c, LANES)


def _s5_branch(u3, kd, sbd, cod, are, aim, d):
    nb, seq, width = u3.shape
    nt = width // LANES
    tt = S5_TT
    kq = S5_Q * LANES
    half = are.shape[-1]
    r = nb * (tt // S5_Q + S5_ROW_PAD)
    tile = pl.BlockSpec((nb, tt, LANES), lambda l, t: (0, t, l))
    per_tile = lambda arr: pl.BlockSpec((1,) + arr.shape[1:], lambda l, t: (l, 0, 0))
    return pl.pallas_call(
        _s5_kernel,
        grid=(nt, seq // tt),
        in_specs=[tile] + [per_tile(arr) for arr in (kd, sbd, cod, are, aim, d)],
        out_specs=tile,
        out_shape=jax.ShapeDtypeStruct(u3.shape, F32),
        scratch_shapes=[pltpu.VMEM((nb, half), F32), pltpu.VMEM((nb, half), F32),
                        pltpu.VMEM((2 * half // LANES, r, LANES), F32),
                        pltpu.VMEM((2 * half // LANES, r, LANES), F32),
                        pltpu.VMEM((kq, kq), BF16), pltpu.VMEM((kq, 2 * half), BF16),
                        pltpu.VMEM((2 * half, kq), BF16)],
        compiler_params=pltpu.CompilerParams(
            dimension_semantics=("arbitrary", "arbitrary"), vmem_limit_bytes=VMEM_LIMIT),
        name="s5_branch",
    )(u3, kd, sbd, cod, are, aim, d)


def _split3(x):
    hi = x.astype(BF16)
    r1 = x - hi.astype(F32)
    mid = r1.astype(BF16)
    lo = (r1 - mid.astype(F32)).astype(BF16)
    return hi, mid, lo


def _project_block(hb_ref, w_ref, dst_ref, src_col, dst_col, n):
    dst_ref[:, dst_col:dst_col + n] = _dot(hb_ref[...], w_ref[:, src_col:src_col + n])


def _proj_ssd_kernel(x_ref, g_ref, w_ref, cw_ref, cb_ref, dtb_ref, a2_ref, dskip_ref, ng_ref,
                     u_ref, ga_ref, gb_ref, y_ref,
                     hb_ref, pad_ref, xc_ref, z_ref, acum_ref, acumt_ref, dtt_ref, wt_ref,
                     ys_ref, st_ref):
    tt, width = y_ref.shape
    q = SSD_Q
    n_state = SSD_STATE
    slabs = pad_ref.shape[0]
    pairs = width // LANES
    groups = (slabs - pairs) // 2
    halo = SUBLANES
    taps = cw_ref.shape[0]
    mxu = 2 * LANES
    o_z = u_ref.shape[-1]
    o_xbc = o_z + width
    o_ga = o_xbc + slabs * LANES
    o_gb = o_ga + ga_ref.shape[-1]
    o_dt = o_gb + gb_ref.shape[-1]

    @pl.when(pl.program_id(1) == 0)
    def _():
        pad_ref[:, tt:tt + halo, :] = jnp.zeros((slabs, halo, LANES), F32)
        st_ref[...] = jnp.zeros_like(st_ref)

    hb_ref[...] = _rms(x_ref[...], g_ref[...]).astype(BF16)

    def project(src_col, n):
        return _dot(hb_ref[...], w_ref[:, src_col:src_col + n])

    row_i = lax.broadcasted_iota(jnp.int32, (q, q), 0)
    col_i = lax.broadcasted_iota(jnp.int32, (q, q), 1)
    causal = row_i >= col_i
    tri = causal.astype(BF16)
    lane = lax.broadcasted_iota(jnp.int32, (q, LANES), 1)
    first_head = lane < SSD_HEADDIM
    first_mask = first_head.astype(BF16)
    second_mask = 1.0 - first_mask

    v = project(o_dt, LANES) + dtb_ref[...]
    dts = jnp.maximum(v, 0.0) + jnp.log1p(jnp.exp(-jnp.abs(v)))
    n_chunks = tt // q
    for ci in range(n_chunks):
        rows = slice(ci * q, (ci + 1) * q)
        dtq = dts[rows]
        hi, mid, lo = _split3(dtq * a2_ref[...])
        acum = _dot(tri, hi) + _dot(tri, mid) + _dot(tri, lo)
        acum_t = acum.T
        dt_t = dtq.T
        acum_ref[rows, :] = acum
        acumt_ref[ci] = acum_t
        dtt_ref[ci] = dt_t
        wt_ref[ci] = jnp.exp2(acum_t[:, q - 1:q] - acum_t) * dt_t

    half = tt // 2
    for blk in range(slabs * LANES // mxu):
        res = project(o_xbc + blk * mxu, mxu)
        for i in range(mxu // LANES):
            n = blk * (mxu // LANES) + i
            cols = slice(n * LANES, (n + 1) * LANES)
            pad_ref[n, 0:halo, :] = pad_ref[n, tt:tt + halo, :]
            pad_ref[n, halo:, :] = res[:, i * LANES:(i + 1) * LANES]
            for s in range(2):
                acc = cb_ref[:, cols]
                for k in range(taps):
                    start = halo - (taps - 1) + s + k
                    acc = acc + cw_ref[k:k + 1, cols] * pad_ref[n, pl.ds(start, half, stride=2), :]
                xc_ref[n, pl.ds(s, half, stride=2), :] = acc + acc * jnp.tanh(acc)

    def deferred(dst_ref, src_col):
        return [functools.partial(_project_block, hb_ref, w_ref, dst_ref, src_col + c, c, mxu)
                for c in range(0, dst_ref.shape[-1], mxu)]

    z_blocks = deferred(z_ref, o_z)
    other_blocks = deferred(u_ref, 0) + deferred(ga_ref, o_ga) + deferred(gb_ref, o_gb)
    slots = n_chunks * pairs
    issue_at = {i: blk for i, blk in enumerate(z_blocks)}
    rest = slots - len(z_blocks)
    for i, blk in enumerate(other_blocks):
        issue_at[len(z_blocks) + (i * rest) // len(other_blocks)] = blk
    assert len(issue_at) == len(z_blocks) + len(other_blocks) <= slots

    for ci in range(n_chunks):
        rows = slice(ci * q, (ci + 1) * q)
        acum = acum_ref[rows, :]
        acum_t = acumt_ref[ci]
        dt_t = dtt_ref[ci]
        w_t = wt_ref[ci]
        for g in range(groups):
            bg = xc_ref[pairs + g, rows, :]
            cg = xc_ref[pairs + groups + g, rows, :].astype(BF16)
            cb = lax.dot_general(cg, bg.astype(BF16), (((1,), (1,)), ((), ())),
                                 preferred_element_type=F32)
            cbm = jnp.where(causal, cb, 0.0).astype(BF16)
            bgt = bg.T.astype(BF16)
            for m in (2 * g, 2 * g + 1):
                if ci * pairs + m in issue_at:
                    issue_at[ci * pairs + m]()
                xs = xc_ref[m, rows, :]
                mats, wts, cols = [], [], []
                for h in (2 * m, 2 * m + 1):
                    col = jnp.broadcast_to(acum[:, h:h + 1], (q, q))
                    cols.append(col)
                    rowv = jnp.broadcast_to(acum_t[h:h + 1, :], (q, q))
                    dtrow = jnp.broadcast_to(dt_t[h:h + 1, :], (q, q)).astype(BF16)
                    decay = jnp.exp2(jnp.minimum(col - rowv, 0.0)).astype(BF16)
                    mats.append(cbm * decay * dtrow)
                    wts.append(bgt * jnp.broadcast_to(w_t[h:h + 1, :], (n_state, q)).astype(BF16))
                mcat = jnp.concatenate(mats, axis=1)
                wcat = jnp.concatenate(wts, axis=1)
                xsb = xs.astype(BF16)
                xbd = jnp.concatenate([xsb * first_mask, xsb * second_mask], axis=0)
                y_diag = _dot(mcat, xbd)
                st_new = _dot(wcat, xbd)
                st_prev = st_ref[m]
                ecol = jnp.exp2(jnp.where(first_head, cols[0], cols[1]))
                y_off = _dot(cg, st_prev.astype(BF16)) * ecol
                st_ref[m] = st_prev * ecol[q - 1:q, :] + st_new
                ys_ref[rows, m * LANES:(m + 1) * LANES] = (
                    y_diag + y_off + xs * dskip_ref[:, m * LANES:(m + 1) * LANES])
        hz = 0.5 * z_ref[rows, :]
        yg = ys_ref[rows, :] * (hz + hz * jnp.tanh(hz))
        y_ref[rows, :] = _rms(yg, ng_ref[...])


def _proj_ssd(x3, g, w, cw, cb, dtb, a2, dskip, ng, s5_width, d_model):
    nb, seq, d_in = x3.shape
    width = ng.shape[-1]
    cdim = cw.shape[-1]
    tt = SSD_TT
    slabs = cdim // LANES
    chunks = tt // SSD_Q
    tile = lambda n: pl.BlockSpec((None, tt, n), lambda b, t: (b, t, 0))
    const = lambda arr: pl.BlockSpec(arr.shape, lambda b, t: (0, 0), pipeline_mode=pl.Buffered(1))
    out_widths = (s5_width, d_model, d_model, width)
    return pl.pallas_call(
        _proj_ssd_kernel,
        grid=(nb, seq // tt),
        in_specs=[tile(d_in)] + [const(arr) for arr in (g, w, cw, cb, dtb, a2, dskip, ng)],
        out_specs=[tile(n) for n in out_widths],
        out_shape=[jax.ShapeDtypeStruct((nb, seq, n), F32) for n in out_widths],
        scratch_shapes=[pltpu.VMEM((tt, d_in), BF16),
                        pltpu.VMEM((slabs, tt + SUBLANES, LANES), F32),
                        pltpu.VMEM((slabs, tt, LANES), F32),
                        pltpu.VMEM((tt, width), F32),
                        pltpu.VMEM((tt, LANES), F32),
                        pltpu.VMEM((chunks, SSD_Q, SSD_Q), F32),
                        pltpu.VMEM((chunks, SSD_Q, SSD_Q), F32),
                        pltpu.VMEM((chunks, SSD_Q, SSD_Q), F32),
                        pltpu.VMEM((tt, width), F32),
                        pltpu.VMEM((width // LANES, SSD_STATE, LANES), F32)],
        compiler_params=pltpu.CompilerParams(
            dimension_semantics=("arbitrary", "arbitrary"), vmem_limit_bytes=VMEM_LIMIT),
        name="proj_ssd",
    )(x3, g, w, cw, cb, dtb, a2, dskip, ng)


def _gelu_tanh(x):
    return 0.5 * x * (1.0 + jnp.tanh(math.sqrt(2.0 / math.pi) * (x + 0.044715 * (x * x * x))))


def _tail_kernel(x_ref, y5_ref, yb_ref, ga_ref, gb_ref, gluw_ref, glub_ref, wa_ref, wb_ref,
                 wo_ref, nm_ref, w1_ref, w2_ref, nf_ref, o_ref, *, apply_final):
    ya = _gelu_tanh(y5_ref[...])
    ya = ya * _sigmoid(_dot(ya.astype(BF16), gluw_ref[...]) + glub_ref[...])
    ya = _dot(ya.astype(BF16), wa_ref[...])
    yb = _dot(yb_ref[...].astype(BF16), wb_ref[...])
    merged = _sigmoid(ga_ref[...]) * ya + _sigmoid(gb_ref[...]) * yb
    x1 = x_ref[...] + _dot(merged.astype(BF16), wo_ref[...])
    hb = _rms(x1, nm_ref[...]).astype(BF16)
    acc = x1
    d_ff = w1_ref.shape[-1]
    ff_block = 1024
    for k in range(d_ff // ff_block):
        cols = slice(k * ff_block, (k + 1) * ff_block)
        act = jnp.maximum(_dot(hb, w1_ref[:, cols]), 0.0)
        acc = acc + _dot((act * act).astype(BF16), w2_ref[cols, :])
    if apply_final:
        acc = _rms(acc, nf_ref[...])
    o_ref[...] = acc


def _tail(x2, y5, yb, ga, gb, gluw, glub, wa, wb, wo, nm, w1, w2, nf, apply_final):
    t, d = x2.shape
    tm = TOKEN_TILE
    row = lambda n: pl.BlockSpec((tm, n), lambda i: (i, 0))
    const = lambda arr: pl.BlockSpec(arr.shape, lambda i: (0, 0), pipeline_mode=pl.Buffered(1))
    consts = (gluw, glub, wa, wb, wo, nm, w1, w2, nf)
    return pl.pallas_call(
        functools.partial(_tail_kernel, apply_final=apply_final),
        grid=(t // tm,),
        in_specs=[row(d), row(y5.shape[-1]), row(yb.shape[-1]), row(d), row(d)]
        + [const(c) for c in consts],
        out_specs=row(d),
        out_shape=jax.ShapeDtypeStruct((t, d), F32),
        compiler_params=pltpu.CompilerParams(
            dimension_semantics=("arbitrary",), vmem_limit_bytes=VMEM_LIMIT),
        name="tail",
    )(x2, y5, yb, ga, gb, *consts)


def kernel(x, norm_mix, w_in, s5_a_re, s5_a_im, s5_log_dt, s5_b_re, s5_b_im, s5_c_re, s5_c_im,
           s5_d, s5_glu_w, s5_glu_b, w_branch_a, ssd_conv_w, ssd_conv_b, ssd_dt_bias, ssd_a_log,
           ssd_d, ssd_norm, w_branch_b, w_out, norm_mlp, w_ff1, w_ff2, norm_final):
    bsz, seq, d_model = x.shape
    depth = w_in.shape[0]
    s5_width = s5_glu_w.shape[-1]
    ssd_width = w_branch_b.shape[1]
    conv_dim = ssd_conv_w.shape[-1]
    heads = ssd_a_log.shape[-1]
    o_u = 0
    o_z = o_u + s5_width
    o_xbc = o_z + ssd_width
    o_dt = o_xbc + conv_dim
    o_ga = o_dt + heads
    o_gb = o_ga + d_model
    row = lambda v: v.reshape(1, -1).astype(F32)
    pad_lanes = lambda v: jnp.pad(v.reshape(1, -1).astype(F32), ((0, 0), (0, LANES - v.shape[-1])))

    x2 = x.reshape(bsz * seq, d_model)
    for i in range(depth):
        w = w_in[i]
        w_perm = jnp.concatenate(
            [w[:, o_u:o_z], w[:, o_z:o_xbc], w[:, o_xbc:o_dt], w[:, o_ga:o_gb], w[:, o_gb:],
             jnp.pad(w[:, o_dt:o_ga], ((0, 0), (0, LANES - heads)))], axis=1).astype(BF16)
        u, ga, gb, yb = _proj_ssd(
            x2.reshape(bsz, seq, d_model), row(norm_mix[i]), w_perm,
            0.5 * ssd_conv_w[i], row(0.5 * ssd_conv_b[i]),
            pad_lanes(ssd_dt_bias[i]), pad_lanes(-jnp.exp(ssd_a_log[i]) * LOG2E),
            row(jnp.repeat(ssd_d[i], SSD_HEADDIM)), row(ssd_norm[i]), s5_width, d_model)

        kd, sbd, cod, are, aim = _s5_operators(s5_a_re[i], s5_a_im[i], s5_log_dt[i], s5_b_re[i],
                                               s5_b_im[i], s5_c_re[i], s5_c_im[i])
        y5 = _s5_branch(u, kd, sbd, cod, are, aim, s5_d[i].reshape(s5_width // LANES, 1, LANES))

        x2 = _tail(x2, y5.reshape(bsz * seq, s5_width), yb.reshape(bsz * seq, ssd_width),
                   ga.reshape(bsz * seq, d_model), gb.reshape(bsz * seq, d_model),
                   s5_glu_w[i].astype(BF16), row(s5_glu_b[i]), w_branch_a[i].astype(BF16),
                   w_branch_b[i].astype(BF16), w_out[i].astype(BF16), row(norm_mlp[i]),
                   w_ff1[i].astype(BF16), w_ff2[i].astype(BF16), row(norm_final),
                   apply_final=(i == depth - 1))
    return x2.reshape(bsz, seq, d_model)
```

```python
import functools
import math

import jax
import jax.numpy as jnp
from jax import lax
from jax.experimental import pallas as pl
from jax.experimental.pallas import tpu as pltpu

F32 = jnp.float32
BF16 = jnp.bfloat16

EPS = 1e-6
LANES = 128
SUBLANES = 8

S5_GROUP = 16
S5_STATE = 64
S5_Q = 8
S5_GROUPS_PER_TILE = LANES // S5_GROUP
S5_TT = 512
S5_ROW_PAD = SUBLANES

SSD_HEADDIM = 64
SSD_STATE = 128
SSD_CONV = 4
SSD_Q = 128
SSD_TT = 512
LOG2E = math.log2(math.e)

TOKEN_TILE = 512
VMEM_LIMIT = 56 * 1024 * 1024


def _rms(x, g):
    return x * lax.rsqrt(jnp.mean(x * x, axis=-1, keepdims=True) + EPS) * g


def _sigmoid(x):
    return 1.0 / (1.0 + jnp.exp(-x))


def _dot(a, b):
    return jnp.dot(a, b, preferred_element_type=F32)


def _s5_operators(a_re, a_im, log_dt, b_re, b_im, c_re, c_im):
    q = S5_Q
    g, p = a_re.shape
    hp = lax.Precision.HIGHEST
    dt = jnp.exp(log_dt)[:, None]
    steps = jnp.arange(q + 1, dtype=F32)[:, None, None]
    mag = jnp.exp(a_re * dt * steps)
    ang = a_im * dt * steps
    pw_re = mag * jnp.cos(ang)
    pw_im = mag * jnp.sin(ang)
    nr = pw_re[1] - 1.0
    ni = pw_im[1]
    den = a_re * a_re + a_im * a_im
    cr = (nr * a_re + ni * a_im) / den
    ci = (ni * a_re - nr * a_im) / den
    nt = g // S5_GROUPS_PER_TILE
    g8 = S5_GROUPS_PER_TILE
    hh = S5_GROUP
    kq = q * LANES
    bt_re = b_re.transpose(0, 2, 1)
    bt_im = b_im.transpose(0, 2, 1)
    bb_re = cr[:, None, :] * bt_re - ci[:, None, :] * bt_im
    bb_im = cr[:, None, :] * bt_im + ci[:, None, :] * bt_re
    cp_re = c_re[None] * pw_re[:, :, None, :] - c_im[None] * pw_im[:, :, None, :]
    cp_im = c_re[None] * pw_im[:, :, None, :] + c_im[None] * pw_re[:, :, None, :]
    kd = (jnp.einsum("jghp,gkp->gkjh", cp_re[:q], bb_re, precision=hp)
          - jnp.einsum("jghp,gkp->gkjh", cp_im[:q], bb_im, precision=hp)).reshape(nt, LANES, q * hh)
    rev_steps = (q - 1) - steps[:q]
    rev_mag = jnp.exp(a_re * dt * rev_steps)
    rev_ang = a_im * dt * rev_steps
    rev_re = (rev_mag * jnp.cos(rev_ang))[:, :, None, :]
    rev_im = (rev_mag * jnp.sin(rev_ang))[:, :, None, :]
    sb = jnp.concatenate([rev_re * bb_re[None] - rev_im * bb_im[None],
                          rev_re * bb_im[None] + rev_im * bb_re[None]], axis=-1)
    sbd = sb.reshape(q, nt, LANES, 2 * p).transpose(1, 0, 2, 3).reshape(nt, kq, 2 * p)
    ct_re = c_re.transpose(0, 2, 1)
    ct_im = c_im.transpose(0, 2, 1)
    co_re = jnp.concatenate([ct_re * pw_re[t + 1][:, :, None] - ct_im * pw_im[t + 1][:, :, None]
                             for t in range(q)], axis=-1)
    co_im = jnp.concatenate([-(ct_re * pw_im[t + 1][:, :, None] + ct_im * pw_re[t + 1][:, :, None])
                             for t in range(q)], axis=-1)
    cod = (jnp.stack([co_re, co_im], axis=0).reshape(2, nt, g8 * p, q * hh)
           .transpose(1, 0, 2, 3).reshape(nt, 2 * g8 * p, q * hh))
    are = pw_re[q].reshape(nt, 1, g8 * p)
    aim = pw_im[q].reshape(nt, 1, g8 * p)
    return kd.astype(BF16), sbd.astype(BF16), cod.astype(BF16), are, aim


def _field(idx, period, width):
    return (idx & (period - 1)) >> (width.bit_length() - 1)


def _s5_expand(compact, rep, row_grp, col_grp):
    wide = _dot(compact, rep)
    return jnp.where(row_grp == col_grp, wide, 0.0).astype(BF16)


def _s5_build_operators(kd_ref, sbd_ref, cod_ref, wi_ref, ws_ref, wo_ref):
    q = S5_Q
    hh = S5_GROUP
    kq = q * LANES
    p = sbd_ref.shape[-1] // 2
    ns = wo_ref.shape[0]
    iota = lambda shape, d: lax.broadcasted_iota(jnp.int32, shape, d)
    r1, c1 = iota((q * hh, kq), 0), iota((q * hh, kq), 1)
    rep_th = ((_field(r1, q * hh, hh) == _field(c1, kq, LANES))
              & (_field(r1, hh, 1) == _field(c1, hh, 1))).astype(BF16)
    r2, c2 = iota((2 * p, ns), 0), iota((2 * p, ns), 1)
    rep_sp = ((_field(r2, 2 * p, p) == _field(c2, ns, ns // 2))
              & (_field(r2, p, 1) == _field(c2, p, 1))).astype(BF16)
    chan_grp = lambda idx: _field(idx, LANES, hh)
    state_grp = lambda idx: _field(idx, ns // 2, p)
    kdx = _s5_expand(kd_ref[0], rep_th, chan_grp(iota((LANES, kq), 0)), chan_grp(iota((LANES, kq), 1)))
    wi_ref[...] = jnp.zeros_like(wi_ref)
    for j in range(q):
        wi_ref[j * LANES:(j + 1) * LANES, j * LANES:] = kdx[:, :(q - j) * LANES]
    ws_ref[...] = _s5_expand(sbd_ref[0], rep_sp, chan_grp(iota((kq, ns), 0)), state_grp(iota((kq, ns), 1)))
    wo_ref[...] = _s5_expand(cod_ref[0], rep_th, state_grp(iota((ns, kq), 0)), chan_grp(iota((ns, kq), 1)))


def _s5_kernel(u_ref, kd_ref, sbd_ref, cod_ref, are_ref, aim_ref, d_ref, y_ref,
               sre_ref, sim_ref, v_ref, sp_ref, yi_ref, wi_ref, ws_ref, wo_ref):
    q = S5_Q
    nb, tt, _ = u_ref.shape
    c = tt // q
    r = nb * c
    half = sre_ref.shape[-1]

    @pl.when(pl.program_id(1) == 0)
    def _():
        sre_ref[...] = jnp.zeros_like(sre_ref)
        sim_ref[...] = jnp.zeros_like(sim_ref)
        _s5_build_operators(kd_ref, sbd_ref, cod_ref, wi_ref, ws_ref, wo_ref)

    us = [u_ref[:, pl.ds(j, c, stride=q), :].reshape(r, LANES) for j in range(q)]
    a = jnp.concatenate([uj.astype(BF16) for uj in us], axis=1)

    v = _dot(a, ws_ref[...])
    slabs = half // LANES
    pitch = v_ref.shape[1] // nb
    for n in range(2 * slabs):
        for b in range(nb):
            v_ref[n, b * pitch:b * pitch + c, :] = v[b * c:(b + 1) * c, n * LANES:(n + 1) * LANES]

    mxu = 2 * LANES
    for n in range(q * LANES // mxu):
        yi_ref[n] = _dot(a[:, :(n + 1) * mxu], wi_ref[:(n + 1) * mxu, n * mxu:(n + 1) * mxu])

    are = jnp.broadcast_to(are_ref[0], (nb, half))
    aim = jnp.broadcast_to(aim_ref[0], (nb, half))
    s_re = sre_ref[...]
    s_im = sim_ref[...]
    for k in range(c):
        step_rows = pl.ds(k, nb, stride=pitch)
        v_re = jnp.concatenate([v_ref[n, step_rows, :] for n in range(slabs)], axis=1)
        v_im = jnp.concatenate([v_ref[slabs + n, step_rows, :] for n in range(slabs)], axis=1)
        for n in range(slabs):
            sp_ref[n, step_rows, :] = s_re[:, n * LANES:(n + 1) * LANES]
            sp_ref[slabs + n, step_rows, :] = s_im[:, n * LANES:(n + 1) * LANES]
        n_re = are * s_re - aim * s_im + v_re
        n_im = are * s_im + aim * s_re + v_im
        s_re, s_im = n_re, n_im
    sre_ref[...] = s_re
    sim_ref[...] = s_im

    spb = jnp.concatenate(
        [jnp.concatenate([sp_ref[n, b * pitch:b * pitch + c, :] for b in range(nb)], axis=0)
         for n in range(2 * slabs)], axis=1).astype(BF16)
    d = d_ref[0]
    steps_per_tile = mxu // LANES
    for n in range(q * LANES // mxu):
        yn = yi_ref[n] + _dot(spb, wo_ref[:, n * mxu:(n + 1) * mxu])
        for s in range(steps_per_tile):
            j = n * steps_per_tile + s
            yj = yn[:, s * LANES:(s + 1) * LANES] + d * us[j]
            y_ref[:, pl.ds(j, c, stride=q), :] = yj.reshape(nb, c, LANES)


def _s5_branch(u3, kd, sbd, cod, are, aim, d):
    nb, seq, width = u3.shape
    nt = width // LANES
    tt = S5_TT
    kq = S5_Q * LANES
    half = are.shape[-1]
    r = nb * (tt // S5_Q + S5_ROW_PAD)
    tile = pl.BlockSpec((nb, tt, LANES), lambda l, t: (0, t, l))
    per_tile = lambda arr: pl.BlockSpec((1,) + arr.shape[1:], lambda l, t: (l, 0, 0))
    return pl.pallas_call(
        _s5_kernel,
        grid=(nt, seq // tt),
        in_specs=[tile] + [per_tile(arr) for arr in (kd, sbd, cod, are, aim, d)],
        out_specs=tile,
        out_shape=jax.ShapeDtypeStruct(u3.shape, F32),
        scratch_shapes=[pltpu.VMEM((nb, half), F32), pltpu.VMEM((nb, half), F32),
                        pltpu.VMEM((2 * half // LANES, r, LANES), F32),
                        pltpu.VMEM((2 * half // LANES, r, LANES), F32),
                        pltpu.VMEM((kq // (2 * LANES), nb * tt // S5_Q, 2 * LANES), F32),
                        pltpu.VMEM((kq, kq), BF16), pltpu.VMEM((kq, 2 * half), BF16),
                        pltpu.VMEM((2 * half, kq), BF16)],
        compiler_params=pltpu.CompilerParams(
            dimension_semantics=("arbitrary", "arbitrary"), vmem_limit_bytes=VMEM_LIMIT),
        name="s5_branch",
    )(u3, kd, sbd, cod, are, aim, d)


def _split3(x):
    hi = x.astype(BF16)
    r1 = x - hi.astype(F32)
    mid = r1.astype(BF16)
    lo = (r1 - mid.astype(F32)).astype(BF16)
    return hi, mid, lo


def _project_block(hb_ref, w_ref, dst_ref, src_col, dst_col, n):
    dst_ref[:, dst_col:dst_col + n] = _dot(hb_ref[...], w_ref[:, src_col:src_col + n])


def _proj_ssd_kernel(x_ref, g_ref, w_ref, cw_ref, cb_ref, dtb_ref, a2_ref, dskip_ref, ng_ref,
                     u_ref, ga_ref, gb_ref, y_ref,
                     hb_ref, pad_ref, xc_ref, z_ref, acum_ref, acumt_ref, dtt_ref, wt_ref,
                     ys_ref, st_ref):
    tt, width = y_ref.shape
    q = SSD_Q
    n_state = SSD_STATE
    slabs = pad_ref.shape[0]
    pairs = width // LANES
    groups = (slabs - pairs) // 2
    halo = SUBLANES
    taps = cw_ref.shape[0]
    mxu = 2 * LANES
    o_z = u_ref.shape[-1]
    o_xbc = o_z + width
    o_ga = o_xbc + slabs * LANES
    o_gb = o_ga + ga_ref.shape[-1]
    o_dt = o_gb + gb_ref.shape[-1]

    @pl.when(pl.program_id(1) == 0)
    def _():
        pad_ref[:, tt:tt + halo, :] = jnp.zeros((slabs, halo, LANES), F32)
        st_ref[...] = jnp.zeros_like(st_ref)

    hb_ref[...] = _rms(x_ref[...], g_ref[...]).astype(BF16)

    def project(src_col, n):
        return _dot(hb_ref[...], w_ref[:, src_col:src_col + n])

    row_i = lax.broadcasted_iota(jnp.int32, (q, q), 0)
    col_i = lax.broadcasted_iota(jnp.int32, (q, q), 1)
    causal = row_i >= col_i
    tri = causal.astype(BF16)
    lane = lax.broadcasted_iota(jnp.int32, (q, LANES), 1)
    first_head = lane < SSD_HEADDIM
    first_mask = first_head.astype(BF16)
    second_mask = 1.0 - first_mask

    v = project(o_dt, LANES) + dtb_ref[...]
    dts = jnp.maximum(v, 0.0) + jnp.log1p(jnp.exp(-jnp.abs(v)))
    n_chunks = tt // q
    for ci in range(n_chunks):
        rows = slice(ci * q, (ci + 1) * q)
        dtq = dts[rows]
        hi, mid, lo = _split3(dtq * a2_ref[...])
        parts = _dot(tri, jnp.concatenate([hi, mid, lo], axis=1))
        acum = (parts[:, :LANES] + parts[:, LANES:2 * LANES]) + parts[:, 2 * LANES:]
        acum_t = acum.T
        dt_t = dtq.T
        acum_ref[rows, :] = acum
        acumt_ref[ci] = acum_t
        dtt_ref[ci] = dt_t
        wt_ref[ci] = jnp.exp2(acum_t[:, q - 1:q] - acum_t) * dt_t

    half = tt // 2
    for blk in range(slabs * LANES // mxu):
        res = project(o_xbc + blk * mxu, mxu)
        for i in range(mxu // LANES):
            n = blk * (mxu // LANES) + i
            cols = slice(n * LANES, (n + 1) * LANES)
            pad_ref[n, 0:halo, :] = pad_ref[n, tt:tt + halo, :]
            pad_ref[n, halo:, :] = res[:, i * LANES:(i + 1) * LANES]
            for s in range(2):
                acc = cb_ref[:, cols]
                for k in range(taps):
                    start = halo - (taps - 1) + s + k
                    acc = acc + cw_ref[k:k + 1, cols] * pad_ref[n, pl.ds(start, half, stride=2), :]
                xc_ref[n, pl.ds(s, half, stride=2), :] = acc + acc * jnp.tanh(acc)

    def deferred(dst_ref, src_col):
        return [functools.partial(_project_block, hb_ref, w_ref, dst_ref, src_col + c, c, mxu)
                for c in range(0, dst_ref.shape[-1], mxu)]

    z_blocks = deferred(z_ref, o_z)
    other_blocks = deferred(u_ref, 0) + deferred(ga_ref, o_ga) + deferred(gb_ref, o_gb)
    slots = n_chunks * pairs
    issue_at = {i: blk for i, blk in enumerate(z_blocks)}
    rest = slots - len(z_blocks)
    for i, blk in enumerate(other_blocks):
        issue_at[len(z_blocks) + (i * rest) // len(other_blocks)] = blk
    assert len(issue_at) == len(z_blocks) + len(other_blocks) <= slots

    for ci in range(n_chunks):
        rows = slice(ci * q, (ci + 1) * q)
        acum = acum_ref[rows, :]
        acum_t = acumt_ref[ci]
        dt_t = dtt_ref[ci]
        w_t = wt_ref[ci]
        for g in range(groups):
            bg = xc_ref[pairs + g, rows, :]
            cg = xc_ref[pairs + groups + g, rows, :].astype(BF16)
            cb = lax.dot_general(cg, bg.astype(BF16), (((1,), (1,)), ((), ())),
                                 preferred_element_type=F32)
            cbm = jnp.where(causal, cb, 0.0).astype(BF16)
            bgt = bg.T.astype(BF16)
            st_prevs = [st_ref[2 * g], st_ref[2 * g + 1]]
            y_offs = _dot(cg, jnp.concatenate([s.astype(BF16) for s in st_prevs], axis=1))
            for m in (2 * g, 2 * g + 1):
                if ci * pairs + m in issue_at:
                    issue_at[ci * pairs + m]()
                xs = xc_ref[m, rows, :]
                mats, wts, cols = [], [], []
                for h in (2 * m, 2 * m + 1):
                    col = jnp.broadcast_to(acum[:, h:h + 1], (q, q))
                    cols.append(col)
                    rowv = jnp.broadcast_to(acum_t[h:h + 1, :], (q, q))
                    dtrow = jnp.broadcast_to(dt_t[h:h + 1, :], (q, q)).astype(BF16)
                    decay = jnp.exp2(jnp.minimum(col - rowv, 0.0)).astype(BF16)
                    mats.append(cbm * decay * dtrow)
                    wts.append(bgt * jnp.broadcast_to(w_t[h:h + 1, :], (n_state, q)).astype(BF16))
                mcat = jnp.concatenate(mats, axis=1)
                wcat = jnp.concatenate(wts, axis=1)
                xsb = xs.astype(BF16)
                xbd = jnp.concatenate([xsb * first_mask, xsb * second_mask], axis=0)
                y_diag = _dot(mcat, xbd)
                st_new = _dot(wcat, xbd)
                st_prev = st_prevs[m - 2 * g]
                ecol = jnp.exp2(jnp.where(first_head, cols[0], cols[1]))
                y_off = y_offs[:, (m - 2 * g) * LANES:(m - 2 * g + 1) * LANES] * ecol
                st_ref[m] = st_prev * ecol[q - 1:q, :] + st_new
                ys_ref[rows, m * LANES:(m + 1) * LANES] = (
                    y_diag + y_off + xs * dskip_ref[:, m * LANES:(m + 1) * LANES])
        hz = 0.5 * z_ref[rows, :]
        yg = ys_ref[rows, :] * (hz + hz * jnp.tanh(hz))
        y_ref[rows, :] = _rms(yg, ng_ref[...])


def _proj_ssd(x3, g, w, cw, cb, dtb, a2, dskip, ng, s5_width, d_model):
    nb, seq, d_in = x3.shape
    width = ng.shape[-1]
    cdim = cw.shape[-1]
    tt = SSD_TT
    slabs = cdim // LANES
    chunks = tt // SSD_Q
    tile = lambda n: pl.BlockSpec((None, tt, n), lambda b, t: (b, t, 0))
    const = lambda arr: pl.BlockSpec(arr.shape, lambda b, t: (0, 0), pipeline_mode=pl.Buffered(1))
    out_widths = (s5_width, d_model, d_model, width)
    return pl.pallas_call(
        _proj_ssd_kernel,
        grid=(nb, seq // tt),
        in_specs=[tile(d_in)] + [const(arr) for arr in (g, w, cw, cb, dtb, a2, dskip, ng)],
        out_specs=[tile(n) for n in out_widths],
        out_shape=[jax.ShapeDtypeStruct((nb, seq, n), F32) for n in out_widths],
        scratch_shapes=[pltpu.VMEM((tt, d_in), BF16),
                        pltpu.VMEM((slabs, tt + SUBLANES, LANES), F32),
                        pltpu.VMEM((slabs, tt, LANES), F32),
                        pltpu.VMEM((tt, width), F32),
                        pltpu.VMEM((tt, LANES), F32),
                        pltpu.VMEM((chunks, SSD_Q, SSD_Q), F32),
                        pltpu.VMEM((chunks, SSD_Q, SSD_Q), F32),
                        pltpu.VMEM((chunks, SSD_Q, SSD_Q), F32),
                        pltpu.VMEM((tt, width), F32),
                        pltpu.VMEM((width // LANES, SSD_STATE, LANES), F32)],
        compiler_params=pltpu.CompilerParams(
            dimension_semantics=("arbitrary", "arbitrary"), vmem_limit_bytes=VMEM_LIMIT),
        name="proj_ssd",
    )(x3, g, w, cw, cb, dtb, a2, dskip, ng)


def _gelu_tanh(x):
    return 0.5 * x * (1.0 + jnp.tanh(math.sqrt(2.0 / math.pi) * (x + 0.044715 * (x * x * x))))


def _tail_kernel(x_ref, y5_ref, yb_ref, ga_ref, gb_ref, gluw_ref, glub_ref, wa_ref, wb_ref,
                 wo_ref, nm_ref, w1_ref, w2_ref, nf_ref, o_ref, *, apply_final):
    ya = _gelu_tanh(y5_ref[...])
    ya = ya * _sigmoid(_dot(ya.astype(BF16), gluw_ref[...]) + glub_ref[...])
    ya = _dot(ya.astype(BF16), wa_ref[...])
    yb = _dot(yb_ref[...].astype(BF16), wb_ref[...])
    merged = _sigmoid(ga_ref[...]) * ya + _sigmoid(gb_ref[...]) * yb
    x1 = x_ref[...] + _dot(merged.astype(BF16), wo_ref[...])
    hb = _rms(x1, nm_ref[...]).astype(BF16)
    acc = x1
    d_ff = w1_ref.shape[-1]
    ff_block = 1024
    for k in range(d_ff // ff_block):
        cols = slice(k * ff_block, (k + 1) * ff_block)
        act = jnp.maximum(_dot(hb, w1_ref[:, cols]), 0.0)
        acc = acc + _dot((act * act).astype(BF16), w2_ref[cols, :])
    if apply_final:
        acc = _rms(acc, nf_ref[...])
    o_ref[...] = acc


def _tail(x2, y5, yb, ga, gb, gluw, glub, wa, wb, wo, nm, w1, w2, nf, apply_final):
    t, d = x2.shape
    tm = TOKEN_TILE
    row = lambda n: pl.BlockSpec((tm, n), lambda i: (i, 0))
    const = lambda arr: pl.BlockSpec(arr.shape, lambda i: (0, 0), pipeline_mode=pl.Buffered(1))
    consts = (gluw, glub, wa, wb, wo, nm, w1, w2, nf)
    return pl.pallas_call(
        functools.partial(_tail_kernel, apply_final=apply_final),
        grid=(t // tm,),
        in_specs=[row(d), row(y5.shape[-1]), row(yb.shape[-1]), row(d), row(d)]
        + [const(c) for c in consts],
        out_specs=row(d),
        out_shape=jax.ShapeDtypeStruct((t, d), F32),
        compiler_params=pltpu.CompilerParams(
            dimension_semantics=("arbitrary",), vmem_limit_bytes=VMEM_LIMIT),
        name="tail",
    )(x2, y5, yb, ga, gb, *consts)


def kernel(x, norm_mix, w_in, s5_a_re, s5_a_im, s5_log_dt, s5_b_re, s5_b_im, s5_c_re, s5_c_im,
           s5_d, s5_glu_w, s5_glu_b, w_branch_a, ssd_conv_w, ssd_conv_b, ssd_dt_bias, ssd_a_log,
           ssd_d, ssd_norm, w_branch_b, w_out, norm_mlp, w_ff1, w_ff2, norm_final):
    bsz, seq, d_model = x.shape
    depth = w_in.shape[0]
    s5_width = s5_glu_w.shape[-1]
    ssd_width = w_branch_b.shape[1]
    conv_dim = ssd_conv_w.shape[-1]
    heads = ssd_a_log.shape[-1]
    o_u = 0
    o_z = o_u + s5_width
    o_xbc = o_z + ssd_width
    o_dt = o_xbc + conv_dim
    o_ga = o_dt + heads
    o_gb = o_ga + d_model
    row = lambda v: v.reshape(1, -1).astype(F32)
    pad_lanes = lambda v: jnp.pad(v.reshape(1, -1).astype(F32), ((0, 0), (0, LANES - v.shape[-1])))

    x2 = x.reshape(bsz * seq, d_model)
    for i in range(depth):
        w = w_in[i]
        w_perm = jnp.concatenate(
            [w[:, o_u:o_z], w[:, o_z:o_xbc], w[:, o_xbc:o_dt], w[:, o_ga:o_gb], w[:, o_gb:],
             jnp.pad(w[:, o_dt:o_ga], ((0, 0), (0, LANES - heads)))], axis=1).astype(BF16)
        u, ga, gb, yb = _proj_ssd(
            x2.reshape(bsz, seq, d_model), row(norm_mix[i]), w_perm,
            0.5 * ssd_conv_w[i], row(0.5 * ssd_conv_b[i]),
            pad_lanes(ssd_dt_bias[i]), pad_lanes(-jnp.exp(ssd_a_log[i]) * LOG2E),
            row(jnp.repeat(ssd_d[i], SSD_HEADDIM)), row(ssd_norm[i]), s5_width, d_model)

        kd, sbd, cod, are, aim = _s5_operators(s5_a_re[i], s5_a_im[i], s5_log_dt[i], s5_b_re[i],
                                               s5_b_im[i], s5_c_re[i], s5_c_im[i])
        y5 = _s5_branch(u, kd, sbd, cod, are, aim, s5_d[i].reshape(s5_width // LANES, 1, LANES))

        x2 = _tail(x2, y5.reshape(bsz * seq, s5_width), yb.reshape(bsz * seq, ssd_width),
                   ga.reshape(bsz * seq, d_model), gb.reshape(bsz * seq, d_model),
                   s5_glu_w[i].astype(BF16), row(s5_glu_b[i]), w_branch_a[i].astype(BF16),
                   w_branch_b[i].astype(BF16), w_out[i].astype(BF16), row(norm_mlp[i]),
                   w_ff1[i].astype(BF16), w_ff2[i].astype(BF16), row(norm_final),
                   apply_final=(i == depth - 1))
    return x2.reshape(bsz, seq, d_model)
```

```python
import functools
import math

import jax
import jax.numpy as jnp
from jax import lax
from jax.experimental import pallas as pl
from jax.experimental.pallas import tpu as pltpu

F32 = jnp.float32
BF16 = jnp.bfloat16

EPS = 1e-6
LANES = 128
SUBLANES = 8

S5_GROUP = 16
S5_STATE = 64
S5_Q = 8
S5_GROUPS_PER_TILE = LANES // S5_GROUP
S5_TT = 512
S5_ROW_PAD = SUBLANES

SSD_HEADDIM = 64
SSD_STATE = 128
SSD_CONV = 4
SSD_Q = 128
SSD_TT = 512
LOG2E = math.log2(math.e)

TOKEN_TILE = 512
VMEM_LIMIT = 56 * 1024 * 1024


def _rms(x, g):
    return x * lax.rsqrt(jnp.mean(x * x, axis=-1, keepdims=True) + EPS) * g


def _sigmoid(x):
    return 1.0 / (1.0 + jnp.exp(-x))


def _dot(a, b):
    return jnp.dot(a, b, preferred_element_type=F32)


def _s5_operators(a_re, a_im, log_dt, b_re, b_im, c_re, c_im):
    q = S5_Q
    g, p = a_re.shape
    dt = jnp.exp(log_dt)[:, None]
    steps = jnp.arange(q + 1, dtype=F32)[:, None, None]
    mag = jnp.exp(a_re * dt * steps)
    ang = a_im * dt * steps
    pw_re = mag * jnp.cos(ang)
    pw_im = mag * jnp.sin(ang)
    nr = pw_re[1] - 1.0
    ni = pw_im[1]
    den = a_re * a_re + a_im * a_im
    cr = (nr * a_re + ni * a_im) / den
    ci = (ni * a_re - nr * a_im) / den
    nt = g // S5_GROUPS_PER_TILE
    g8 = S5_GROUPS_PER_TILE
    hh = S5_GROUP
    kq = q * LANES
    bt_re = b_re.transpose(0, 2, 1)
    bt_im = b_im.transpose(0, 2, 1)
    bb_re = cr[:, None, :] * bt_re - ci[:, None, :] * bt_im
    bb_im = cr[:, None, :] * bt_im + ci[:, None, :] * bt_re
    cp_re = c_re[None] * pw_re[:, :, None, :] - c_im[None] * pw_im[:, :, None, :]
    cp_im = c_re[None] * pw_im[:, :, None, :] + c_im[None] * pw_re[:, :, None, :]
    kj = jnp.sum(cp_re[:q, :, :, None, :] * bb_re[None, :, None, :, :]
                 - cp_im[:q, :, :, None, :] * bb_im[None, :, None, :, :], axis=-1)
    kd = kj.transpose(1, 3, 0, 2).reshape(nt, LANES, q * hh)
    rev_steps = (q - 1) - steps[:q]
    rev_mag = jnp.exp(a_re * dt * rev_steps)
    rev_ang = a_im * dt * rev_steps
    rev_re = (rev_mag * jnp.cos(rev_ang))[:, :, None, :]
    rev_im = (rev_mag * jnp.sin(rev_ang))[:, :, None, :]
    sb = jnp.concatenate([rev_re * bb_re[None] - rev_im * bb_im[None],
                          rev_re * bb_im[None] + rev_im * bb_re[None]], axis=-1)
    sbd = sb.reshape(q, nt, LANES, 2 * p).transpose(1, 0, 2, 3).reshape(nt, kq, 2 * p)
    ct_re = c_re.transpose(0, 2, 1)
    ct_im = c_im.transpose(0, 2, 1)
    co_re = jnp.concatenate([ct_re * pw_re[t + 1][:, :, None] - ct_im * pw_im[t + 1][:, :, None]
                             for t in range(q)], axis=-1)
    co_im = jnp.concatenate([-(ct_re * pw_im[t + 1][:, :, None] + ct_im * pw_re[t + 1][:, :, None])
                             for t in range(q)], axis=-1)
    cod = (jnp.stack([co_re, co_im], axis=0).reshape(2, nt, g8 * p, q * hh)
           .transpose(1, 0, 2, 3).reshape(nt, 2 * g8 * p, q * hh))
    are = pw_re[q].reshape(nt, 1, g8 * p)
    aim = pw_im[q].reshape(nt, 1, g8 * p)
    return kd.astype(BF16), sbd.astype(BF16), cod.astype(BF16), are, aim


def _field(idx, period, width):
    return (idx & (period - 1)) >> (width.bit_length() - 1)


def _s5_expand(compact, rep, row_grp, col_grp):
    wide = _dot(compact, rep)
    return jnp.where(row_grp == col_grp, wide, 0.0).astype(BF16)


def _s5_build_operators(kd_ref, sbd_ref, cod_ref, wi_ref, ws_ref, wo_ref):
    q = S5_Q
    hh = S5_GROUP
    kq = q * LANES
    p = sbd_ref.shape[-1] // 2
    ns = wo_ref.shape[0]
    iota = lambda shape, d: lax.broadcasted_iota(jnp.int32, shape, d)
    r1, c1 = iota((q * hh, kq), 0), iota((q * hh, kq), 1)
    rep_th = ((_field(r1, q * hh, hh) == _field(c1, kq, LANES))
              & (_field(r1, hh, 1) == _field(c1, hh, 1))).astype(BF16)
    r2, c2 = iota((2 * p, ns), 0), iota((2 * p, ns), 1)
    rep_sp = ((_field(r2, 2 * p, p) == _field(c2, ns, ns // 2))
              & (_field(r2, p, 1) == _field(c2, p, 1))).astype(BF16)
    chan_grp = lambda idx: _field(idx, LANES, hh)
    state_grp = lambda idx: _field(idx, ns // 2, p)
    kdx = _s5_expand(kd_ref[0], rep_th, chan_grp(iota((LANES, kq), 0)), chan_grp(iota((LANES, kq), 1)))
    wi_ref[...] = jnp.zeros_like(wi_ref)
    for j in range(q):
        wi_ref[j * LANES:(j + 1) * LANES, j * LANES:] = kdx[:, :(q - j) * LANES]
    ws_ref[...] = _s5_expand(sbd_ref[0], rep_sp, chan_grp(iota((kq, ns), 0)), state_grp(iota((kq, ns), 1)))
    wo_ref[...] = _s5_expand(cod_ref[0], rep_th, state_grp(iota((ns, kq), 0)), chan_grp(iota((ns, kq), 1)))


def _s5_kernel(u_ref, kd_ref, sbd_ref, cod_ref, are_ref, aim_ref, d_ref, y_ref,
               sre_ref, sim_ref, v_ref, sp_ref, yi_ref, wi_ref, ws_ref, wo_ref):
    q = S5_Q
    nb, tt, _ = u_ref.shape
    c = tt // q
    r = nb * c
    half = sre_ref.shape[-1]

    @pl.when(pl.program_id(1) == 0)
    def _():
        sre_ref[...] = jnp.zeros_like(sre_ref)
        sim_ref[...] = jnp.zeros_like(sim_ref)
        _s5_build_operators(kd_ref, sbd_ref, cod_ref, wi_ref, ws_ref, wo_ref)

    us = [u_ref[:, pl.ds(j, c, stride=q), :].reshape(r, LANES) for j in range(q)]
    a = jnp.concatenate([uj.astype(BF16) for uj in us], axis=1)

    v = _dot(a, ws_ref[...])
    slabs = half // LANES
    pitch = v_ref.shape[1] // nb
    for n in range(2 * slabs):
        for b in range(nb):
            v_ref[n, b * pitch:b * pitch + c, :] = v[b * c:(b + 1) * c, n * LANES:(n + 1) * LANES]

    mxu = 2 * LANES
    for n in range(q * LANES // mxu):
        yi_ref[n] = _dot(a[:, :(n + 1) * mxu], wi_ref[:(n + 1) * mxu, n * mxu:(n + 1) * mxu])

    are = jnp.broadcast_to(are_ref[0], (nb, half))
    aim = jnp.broadcast_to(aim_ref[0], (nb, half))
    s_re = sre_ref[...]
    s_im = sim_ref[...]
    for k in range(c):
        step_rows = pl.ds(k, nb, stride=pitch)
        v_re = jnp.concatenate([v_ref[n, step_rows, :] for n in range(slabs)], axis=1)
        v_im = jnp.concatenate([v_ref[slabs + n, step_rows, :] for n in range(slabs)], axis=1)
        for n in range(slabs):
            sp_ref[n, step_rows, :] = s_re[:, n * LANES:(n + 1) * LANES]
            sp_ref[slabs + n, step_rows, :] = s_im[:, n * LANES:(n + 1) * LANES]
        n_re = are * s_re - aim * s_im + v_re
        n_im = are * s_im + aim * s_re + v_im
        s_re, s_im = n_re, n_im
    sre_ref[...] = s_re
    sim_ref[...] = s_im

    spb = jnp.concatenate(
        [jnp.concatenate([sp_ref[n, b * pitch:b * pitch + c, :] for b in range(nb)], axis=0)
         for n in range(2 * slabs)], axis=1).astype(BF16)
    d = d_ref[0]
    steps_per_tile = mxu // LANES
    for n in range(q * LANES // mxu):
        yn = yi_ref[n] + _dot(spb, wo_ref[:, n * mxu:(n + 1) * mxu])
        for s in range(steps_per_tile):
            j = n * steps_per_tile + s
            yj = yn[:, s * LANES:(s + 1) * LANES] + d * us[j]
            y_ref[:, pl.ds(j, c, stride=q), :] = yj.reshape(nb, c, LANES)


def _s5_branch(u3, kd, sbd, cod, are, aim, d):
    nb, seq, width = u3.shape
    nt = width // LANES
    tt = S5_TT
    kq = S5_Q * LANES
    half = are.shape[-1]
    r = nb * (tt // S5_Q + S5_ROW_PAD)
    tile = pl.BlockSpec((nb, tt, LANES), lambda l, t: (0, t, l))
    per_tile = lambda arr: pl.BlockSpec((1,) + arr.shape[1:], lambda l, t: (l, 0, 0))
    return pl.pallas_call(
        _s5_kernel,
        grid=(nt, seq // tt),
        in_specs=[tile] + [per_tile(arr) for arr in (kd, sbd, cod, are, aim, d)],
        out_specs=tile,
        out_shape=jax.ShapeDtypeStruct(u3.shape, F32),
        scratch_shapes=[pltpu.VMEM((nb, half), F32), pltpu.VMEM((nb, half), F32),
                        pltpu.VMEM((2 * half // LANES, r, LANES), F32),
                        pltpu.VMEM((2 * half // LANES, r, LANES), F32),
                        pltpu.VMEM((kq // (2 * LANES), nb * tt // S5_Q, 2 * LANES), F32),
                        pltpu.VMEM((kq, kq), BF16), pltpu.VMEM((kq, 2 * half), BF16),
                        pltpu.VMEM((2 * half, kq), BF16)],
        compiler_params=pltpu.CompilerParams(
            dimension_semantics=("arbitrary", "arbitrary"), vmem_limit_bytes=VMEM_LIMIT),
        name="s5_branch",
    )(u3, kd, sbd, cod, are, aim, d)


def _split3(x):
    hi = x.astype(BF16)
    r1 = x - hi.astype(F32)
    mid = r1.astype(BF16)
    lo = (r1 - mid.astype(F32)).astype(BF16)
    return hi, mid, lo


def _project_block(hb_ref, w_ref, dst_ref, src_col, dst_col, n):
    dst_ref[:, dst_col:dst_col + n] = _dot(hb_ref[...], w_ref[:, src_col:src_col + n])


def _proj_ssd_kernel(x_ref, g_ref, w_ref, wg_ref, cw_ref, cb_ref, dtb_ref, a2_ref, dskip_ref,
                     ng_ref, u_ref, ga_ref, gb_ref, y_ref,
                     hb_ref, pad_ref, xc_ref, z_ref, acum_ref, acumt_ref, dtt_ref, wt_ref,
                     ys_ref, st_ref):
    tt, width = y_ref.shape
    q = SSD_Q
    n_state = SSD_STATE
    slabs = pad_ref.shape[0]
    pairs = width // LANES
    groups = (slabs - pairs) // 2
    halo = SUBLANES
    taps = cw_ref.shape[0]
    mxu = 2 * LANES
    o_z = u_ref.shape[-1]
    o_xbc = o_z + width
    o_dt = o_xbc + slabs * LANES

    @pl.when(pl.program_id(1) == 0)
    def _():
        pad_ref[:, tt:tt + halo, :] = jnp.zeros((slabs, halo, LANES), F32)
        st_ref[...] = jnp.zeros_like(st_ref)

    hb_ref[...] = _rms(x_ref[...], g_ref[...]).astype(BF16)

    def project(src_col, n):
        return _dot(hb_ref[...], w_ref[:, src_col:src_col + n])

    row_i = lax.broadcasted_iota(jnp.int32, (q, q), 0)
    col_i = lax.broadcasted_iota(jnp.int32, (q, q), 1)
    causal = row_i >= col_i
    tri = causal.astype(BF16)
    lane = lax.broadcasted_iota(jnp.int32, (q, LANES), 1)
    first_head = lane < SSD_HEADDIM
    first_mask = first_head.astype(BF16)
    second_mask = 1.0 - first_mask

    v = project(o_dt, LANES) + dtb_ref[...]
    dts = jnp.maximum(v, 0.0) + jnp.log1p(jnp.exp(-jnp.abs(v)))
    n_chunks = tt // q
    for ci in range(n_chunks):
        rows = slice(ci * q, (ci + 1) * q)
        dtq = dts[rows]
        hi, mid, lo = _split3(dtq * a2_ref[...])
        parts = _dot(tri, jnp.concatenate([hi, mid, lo], axis=1))
        acum = (parts[:, :LANES] + parts[:, LANES:2 * LANES]) + parts[:, 2 * LANES:]
        acum_t = acum.T
        dt_t = dtq.T
        acum_ref[rows, :] = acum
        acumt_ref[ci] = acum_t
        dtt_ref[ci] = dt_t
        wt_ref[ci] = jnp.exp2(acum_t[:, q - 1:q] - acum_t) * dt_t

    half = tt // 2
    for blk in range(slabs * LANES // mxu):
        res = project(o_xbc + blk * mxu, mxu)
        for i in range(mxu // LANES):
            n = blk * (mxu // LANES) + i
            cols = slice(n * LANES, (n + 1) * LANES)
            pad_ref[n, 0:halo, :] = pad_ref[n, tt:tt + halo, :]
            pad_ref[n, halo:, :] = res[:, i * LANES:(i + 1) * LANES]
            for s in range(2):
                acc = cb_ref[:, cols]
                for k in range(taps):
                    start = halo - (taps - 1) + s + k
                    acc = acc + cw_ref[k:k + 1, cols] * pad_ref[n, pl.ds(start, half, stride=2), :]
                xc_ref[n, pl.ds(s, half, stride=2), :] = acc + acc * jnp.tanh(acc)

    def deferred(dst_ref, src_ref, src_col):
        return [functools.partial(_project_block, hb_ref, src_ref, dst_ref, src_col + c, c, mxu)
                for c in range(0, dst_ref.shape[-1], mxu)]

    z_blocks = deferred(z_ref, w_ref, o_z)
    other_blocks = (deferred(u_ref, w_ref, 0) + deferred(ga_ref, wg_ref, 0)
                    + deferred(gb_ref, wg_ref, ga_ref.shape[-1]))
    slots = n_chunks * pairs
    issue_at = {i: blk for i, blk in enumerate(z_blocks)}
    rest = slots - len(z_blocks)
    for i, blk in enumerate(other_blocks):
        issue_at[len(z_blocks) + (i * rest) // len(other_blocks)] = blk
    assert len(issue_at) == len(z_blocks) + len(other_blocks) <= slots

    for ci in range(n_chunks):
        rows = slice(ci * q, (ci + 1) * q)
        acum = acum_ref[rows, :]
        acum_t = acumt_ref[ci]
        dt_t = dtt_ref[ci]
        w_t = wt_ref[ci]
        for g in range(groups):
            bg = xc_ref[pairs + g, rows, :]
            cg = xc_ref[pairs + groups + g, rows, :].astype(BF16)
            cb = lax.dot_general(cg, bg.astype(BF16), (((1,), (1,)), ((), ())),
                                 preferred_element_type=F32)
            cbm = jnp.where(causal, cb, 0.0).astype(BF16)
            bgt = bg.T.astype(BF16)
            st_prevs = [st_ref[2 * g], st_ref[2 * g + 1]]
            y_offs = _dot(cg, jnp.concatenate([s.astype(BF16) for s in st_prevs], axis=1))
            for m in (2 * g, 2 * g + 1):
                if ci * pairs + m in issue_at:
                    issue_at[ci * pairs + m]()
                xs = xc_ref[m, rows, :]
                mats, wts, cols = [], [], []
                for h in (2 * m, 2 * m + 1):
                    col = jnp.broadcast_to(acum[:, h:h + 1], (q, q))
                    cols.append(col)
                    rowv = jnp.broadcast_to(acum_t[h:h + 1, :], (q, q))
                    dtrow = jnp.broadcast_to(dt_t[h:h + 1, :], (q, q)).astype(BF16)
                    decay = jnp.exp2(jnp.minimum(col - rowv, 0.0)).astype(BF16)
                    mats.append(cbm * decay * dtrow)
                    wts.append(bgt * jnp.broadcast_to(w_t[h:h + 1, :], (n_state, q)).astype(BF16))
                mcat = jnp.concatenate(mats, axis=1)
                wcat = jnp.concatenate(wts, axis=1)
                xsb = xs.astype(BF16)
                xbd = jnp.concatenate([xsb * first_mask, xsb * second_mask], axis=0)
                y_diag = _dot(mcat, xbd)
                st_new = _dot(wcat, xbd)
                st_prev = st_prevs[m - 2 * g]
                ecol = jnp.exp2(jnp.where(first_head, cols[0], cols[1]))
                y_off = y_offs[:, (m - 2 * g) * LANES:(m - 2 * g + 1) * LANES] * ecol
                st_ref[m] = st_prev * ecol[q - 1:q, :] + st_new
                ys_ref[rows, m * LANES:(m + 1) * LANES] = (
                    y_diag + y_off + xs * dskip_ref[:, m * LANES:(m + 1) * LANES])
        hz = 0.5 * z_ref[rows, :]
        yg = ys_ref[rows, :] * (hz + hz * jnp.tanh(hz))
        y_ref[rows, :] = _rms(yg, ng_ref[...])


def _proj_ssd(x3, g, w, wg, cw, cb, dtb, a2, dskip, ng, s5_width, d_model):
    nb, seq, d_in = x3.shape
    width = ng.shape[-1]
    cdim = cw.shape[-1]
    tt = SSD_TT
    slabs = cdim // LANES
    chunks = tt // SSD_Q
    tile = lambda n: pl.BlockSpec((None, tt, n), lambda b, t: (b, t, 0))
    const = lambda arr: pl.BlockSpec(arr.shape, lambda b, t: (0, 0), pipeline_mode=pl.Buffered(1))
    out_widths = (s5_width, d_model, d_model, width)
    return pl.pallas_call(
        _proj_ssd_kernel,
        grid=(nb, seq // tt),
        in_specs=[tile(d_in)] + [const(arr) for arr in (g, w, wg, cw, cb, dtb, a2, dskip, ng)],
        out_specs=[tile(n) for n in out_widths],
        out_shape=[jax.ShapeDtypeStruct((nb, seq, n), F32) for n in out_widths],
        scratch_shapes=[pltpu.VMEM((tt, d_in), BF16),
                        pltpu.VMEM((slabs, tt + SUBLANES, LANES), F32),
                        pltpu.VMEM((slabs, tt, LANES), F32),
                        pltpu.VMEM((tt, width), F32),
                        pltpu.VMEM((tt, LANES), F32),
                        pltpu.VMEM((chunks, SSD_Q, SSD_Q), F32),
                        pltpu.VMEM((chunks, SSD_Q, SSD_Q), F32),
                        pltpu.VMEM((chunks, SSD_Q, SSD_Q), F32),
                        pltpu.VMEM((tt, width), F32),
                        pltpu.VMEM((width // LANES, SSD_STATE, LANES), F32)],
        compiler_params=pltpu.CompilerParams(
            dimension_semantics=("arbitrary", "arbitrary"), vmem_limit_bytes=VMEM_LIMIT),
        name="proj_ssd",
    )(x3, g, w, wg, cw, cb, dtb, a2, dskip, ng)


def _gelu_tanh(x):
    return 0.5 * x * (1.0 + jnp.tanh(math.sqrt(2.0 / math.pi) * (x + 0.044715 * (x * x * x))))


def _tail_kernel(x_ref, y5_ref, yb_ref, ga_ref, gb_ref, gluw_ref, glub_ref, wa_ref, wb_ref,
                 wo_ref, nm_ref, w1_ref, w2_ref, nf_ref, o_ref, *, apply_final):
    ya = _gelu_tanh(y5_ref[...])
    ya = ya * _sigmoid(_dot(ya.astype(BF16), gluw_ref[...]) + glub_ref[...])
    ya = _dot(ya.astype(BF16), wa_ref[...])
    yb = _dot(yb_ref[...].astype(BF16), wb_ref[...])
    merged = _sigmoid(ga_ref[...]) * ya + _sigmoid(gb_ref[...]) * yb
    x1 = x_ref[...] + _dot(merged.astype(BF16), wo_ref[...])
    hb = _rms(x1, nm_ref[...]).astype(BF16)
    acc = x1
    d_ff = w1_ref.shape[-1]
    ff_block = 1024
    for k in range(d_ff // ff_block):
        cols = slice(k * ff_block, (k + 1) * ff_block)
        act = jnp.maximum(_dot(hb, w1_ref[:, cols]), 0.0)
        acc = acc + _dot((act * act).astype(BF16), w2_ref[cols, :])
    if apply_final:
        acc = _rms(acc, nf_ref[...])
    o_ref[...] = acc


def _tail(x2, y5, yb, ga, gb, gluw, glub, wa, wb, wo, nm, w1, w2, nf, apply_final):
    t, d = x2.shape
    tm = TOKEN_TILE
    row = lambda n: pl.BlockSpec((tm, n), lambda i: (i, 0))
    const = lambda arr: pl.BlockSpec(arr.shape, lambda i: (0, 0), pipeline_mode=pl.Buffered(1))
    consts = (gluw, glub, wa, wb, wo, nm, w1, w2, nf)
    return pl.pallas_call(
        functools.partial(_tail_kernel, apply_final=apply_final),
        grid=(t // tm,),
        in_specs=[row(d), row(y5.shape[-1]), row(yb.shape[-1]), row(d), row(d)]
        + [const(c) for c in consts],
        out_specs=row(d),
        out_shape=jax.ShapeDtypeStruct((t, d), F32),
        compiler_params=pltpu.CompilerParams(
            dimension_semantics=("arbitrary",), vmem_limit_bytes=VMEM_LIMIT),
        name="tail",
    )(x2, y5, yb, ga, gb, *consts)


def kernel(x, norm_mix, w_in, s5_a_re, s5_a_im, s5_log_dt, s5_b_re, s5_b_im, s5_c_re, s5_c_im,
           s5_d, s5_glu_w, s5_glu_b, w_branch_a, ssd_conv_w, ssd_conv_b, ssd_dt_bias, ssd_a_log,
           ssd_d, ssd_norm, w_branch_b, w_out, norm_mlp, w_ff1, w_ff2, norm_final):
    bsz, seq, d_model = x.shape
    depth = w_in.shape[0]
    s5_width = s5_glu_w.shape[-1]
    ssd_width = w_branch_b.shape[1]
    conv_dim = ssd_conv_w.shape[-1]
    heads = ssd_a_log.shape[-1]
    o_dt = s5_width + ssd_width + conv_dim
    o_ga = o_dt + heads
    assert o_dt % LANES == 0 and heads <= LANES and o_dt + LANES <= w_in.shape[-1]
    row = lambda v: v.reshape(1, -1).astype(F32)
    pad_lanes = lambda v: jnp.pad(v.reshape(1, -1).astype(F32), ((0, 0), (0, LANES - v.shape[-1])))

    x2 = x.reshape(bsz * seq, d_model)
    for i in range(depth):
        u, ga, gb, yb = _proj_ssd(
            x2.reshape(bsz, seq, d_model), row(norm_mix[i]),
            w_in[i].astype(BF16), w_in[i, :, o_ga:].astype(BF16),
            0.5 * ssd_conv_w[i], row(0.5 * ssd_conv_b[i]),
            pad_lanes(ssd_dt_bias[i]), pad_lanes(-jnp.exp(ssd_a_log[i]) * LOG2E),
            row(jnp.repeat(ssd_d[i], SSD_HEADDIM)), row(ssd_norm[i]), s5_width, d_model)

        kd, sbd, cod, are, aim = _s5_operators(s5_a_re[i], s5_a_im[i], s5_log_dt[i], s5_b_re[i],
                                               s5_b_im[i], s5_c_re[i], s5_c_im[i])
        y5 = _s5_branch(u, kd, sbd, cod, are, aim, s5_d[i].reshape(s5_width // LANES, 1, LANES))

        x2 = _tail(x2, y5.reshape(bsz * seq, s5_width), yb.reshape(bsz * seq, ssd_width),
                   ga.reshape(bsz * seq, d_model), gb.reshape(bsz * seq, d_model),
                   s5_glu_w[i].astype(BF16), row(s5_glu_b[i]), w_branch_a[i].astype(BF16),
                   w_branch_b[i].astype(BF16), w_out[i].astype(BF16), row(norm_mlp[i]),
                   w_ff1[i].astype(BF16), w_ff2[i].astype(BF16), row(norm_final),
                   apply_final=(i == depth - 1))
    return x2.reshape(bsz, seq, d_model)
```

```python
import functools
import math

import jax
import jax.numpy as jnp
from jax import lax
from jax.experimental import pallas as pl
from jax.experimental.pallas import tpu as pltpu

F32 = jnp.float32
BF16 = jnp.bfloat16

EPS = 1e-6
LANES = 128
SUBLANES = 8

S5_GROUP = 16
S5_STATE = 64
S5_Q = 8
S5_GROUPS_PER_TILE = LANES // S5_GROUP
S5_TT = 512
S5_ROW_PAD = SUBLANES

SSD_HEADDIM = 64
SSD_STATE = 128
SSD_CONV = 4
SSD_Q = 128
SSD_TT = 512
TABLES_AFTER_BLOCK = 2
LOG2E = math.log2(math.e)

TOKEN_TILE = 512
VMEM_LIMIT = 56 * 1024 * 1024


def _rms(x, g):
    return x * lax.rsqrt(jnp.mean(x * x, axis=-1, keepdims=True) + EPS) * g


def _sigmoid(x):
    return 1.0 / (1.0 + jnp.exp(-x))


def _dot(a, b):
    return jnp.dot(a, b, preferred_element_type=F32)


def _s5_operators(a_re, a_im, log_dt, b_re, b_im, c_re, c_im):
    q = S5_Q
    g, p = a_re.shape
    dt = jnp.exp(log_dt)[:, None]
    steps = jnp.arange(q + 1, dtype=F32)[:, None, None]
    mag = jnp.exp(a_re * dt * steps)
    ang = a_im * dt * steps
    pw_re = mag * jnp.cos(ang)
    pw_im = mag * jnp.sin(ang)
    nr = pw_re[1] - 1.0
    ni = pw_im[1]
    den = a_re * a_re + a_im * a_im
    cr = (nr * a_re + ni * a_im) / den
    ci = (ni * a_re - nr * a_im) / den
    nt = g // S5_GROUPS_PER_TILE
    g8 = S5_GROUPS_PER_TILE
    hh = S5_GROUP
    kq = q * LANES
    bt_re = b_re.transpose(0, 2, 1)
    bt_im = b_im.transpose(0, 2, 1)
    bb_re = cr[:, None, :] * bt_re - ci[:, None, :] * bt_im
    bb_im = cr[:, None, :] * bt_im + ci[:, None, :] * bt_re
    cp_re = c_re[None] * pw_re[:, :, None, :] - c_im[None] * pw_im[:, :, None, :]
    cp_im = c_re[None] * pw_im[:, :, None, :] + c_im[None] * pw_re[:, :, None, :]
    kj = jnp.sum(cp_re[:q, :, :, None, :] * bb_re[None, :, None, :, :]
                 - cp_im[:q, :, :, None, :] * bb_im[None, :, None, :, :], axis=-1)
    kd = kj.transpose(1, 3, 0, 2).reshape(nt, LANES, q * hh)
    rev_steps = (q - 1) - steps[:q]
    rev_mag = jnp.exp(a_re * dt * rev_steps)
    rev_ang = a_im * dt * rev_steps
    rev_re = (rev_mag * jnp.cos(rev_ang))[:, :, None, :]
    rev_im = (rev_mag * jnp.sin(rev_ang))[:, :, None, :]
    sb = jnp.concatenate([rev_re * bb_re[None] - rev_im * bb_im[None],
                          rev_re * bb_im[None] + rev_im * bb_re[None]], axis=-1)
    sbd = sb.reshape(q, nt, LANES, 2 * p).transpose(1, 0, 2, 3).reshape(nt, kq, 2 * p)
    ct_re = c_re.transpose(0, 2, 1)
    ct_im = c_im.transpose(0, 2, 1)
    co_re = jnp.concatenate([ct_re * pw_re[t + 1][:, :, None] - ct_im * pw_im[t + 1][:, :, None]
                             for t in range(q)], axis=-1)
    co_im = jnp.concatenate([-(ct_re * pw_im[t + 1][:, :, None] + ct_im * pw_re[t + 1][:, :, None])
                             for t in range(q)], axis=-1)
    cod = (jnp.stack([co_re, co_im], axis=0).reshape(2, nt, g8 * p, q * hh)
           .transpose(1, 0, 2, 3).reshape(nt, 2 * g8 * p, q * hh))
    are = pw_re[q].reshape(nt, 1, g8 * p)
    aim = pw_im[q].reshape(nt, 1, g8 * p)
    return kd.astype(BF16), sbd.astype(BF16), cod.astype(BF16), are, aim


def _field(idx, period, width):
    return (idx & (period - 1)) >> (width.bit_length() - 1)


def _s5_expand(compact, rep, row_grp, col_grp):
    wide = _dot(compact, rep)
    return jnp.where(row_grp == col_grp, wide, 0.0).astype(BF16)


def _s5_build_operators(kd_ref, sbd_ref, cod_ref, wi_ref, ws_ref, wo_ref):
    q = S5_Q
    hh = S5_GROUP
    kq = q * LANES
    p = sbd_ref.shape[-1] // 2
    ns = wo_ref.shape[0]
    iota = lambda shape, d: lax.broadcasted_iota(jnp.int32, shape, d)
    r1, c1 = iota((q * hh, kq), 0), iota((q * hh, kq), 1)
    rep_th = ((_field(r1, q * hh, hh) == _field(c1, kq, LANES))
              & (_field(r1, hh, 1) == _field(c1, hh, 1))).astype(BF16)
    r2, c2 = iota((2 * p, ns), 0), iota((2 * p, ns), 1)
    rep_sp = ((_field(r2, 2 * p, p) == _field(c2, ns, ns // 2))
              & (_field(r2, p, 1) == _field(c2, p, 1))).astype(BF16)
    chan_grp = lambda idx: _field(idx, LANES, hh)
    state_grp = lambda idx: _field(idx, ns // 2, p)
    kdx = _s5_expand(kd_ref[0], rep_th, chan_grp(iota((LANES, kq), 0)), chan_grp(iota((LANES, kq), 1)))
    wi_ref[...] = jnp.zeros_like(wi_ref)
    for j in range(q):
        wi_ref[j * LANES:(j + 1) * LANES, j * LANES:] = kdx[:, :(q - j) * LANES]
    ws_ref[...] = _s5_expand(sbd_ref[0], rep_sp, chan_grp(iota((kq, ns), 0)), state_grp(iota((kq, ns), 1)))
    wo_ref[...] = _s5_expand(cod_ref[0], rep_th, state_grp(iota((ns, kq), 0)), chan_grp(iota((ns, kq), 1)))


def _s5_kernel(u_ref, kd_ref, sbd_ref, cod_ref, are_ref, aim_ref, d_ref, y_ref,
               sre_ref, sim_ref, v_ref, sp_ref, yi_ref, wi_ref, ws_ref, wo_ref):
    q = S5_Q
    nb, tt, _ = u_ref.shape
    c = tt // q
    r = nb * c
    half = sre_ref.shape[-1]

    @pl.when(pl.program_id(1) == 0)
    def _():
        sre_ref[...] = jnp.zeros_like(sre_ref)
        sim_ref[...] = jnp.zeros_like(sim_ref)
        _s5_build_operators(kd_ref, sbd_ref, cod_ref, wi_ref, ws_ref, wo_ref)

    us = [u_ref[:, pl.ds(j, c, stride=q), :].reshape(r, LANES) for j in range(q)]
    a = jnp.concatenate([uj.astype(BF16) for uj in us], axis=1)

    v = _dot(a, ws_ref[...])
    slabs = half // LANES
    pitch = v_ref.shape[1] // nb
    for n in range(2 * slabs):
        for b in range(nb):
            v_ref[n, b * pitch:b * pitch + c, :] = v[b * c:(b + 1) * c, n * LANES:(n + 1) * LANES]

    mxu = 2 * LANES
    for n in range(q * LANES // mxu):
        yi_ref[n] = _dot(a[:, :(n + 1) * mxu], wi_ref[:(n + 1) * mxu, n * mxu:(n + 1) * mxu])

    are = jnp.broadcast_to(are_ref[0], (nb, half))
    aim = jnp.broadcast_to(aim_ref[0], (nb, half))
    s_re = sre_ref[...]
    s_im = sim_ref[...]
    for k in range(c):
        step_rows = pl.ds(k, nb, stride=pitch)
        v_re = jnp.concatenate([v_ref[n, step_rows, :] for n in range(slabs)], axis=1)
        v_im = jnp.concatenate([v_ref[slabs + n, step_rows, :] for n in range(slabs)], axis=1)
        for n in range(slabs):
            sp_ref[n, step_rows, :] = s_re[:, n * LANES:(n + 1) * LANES]
            sp_ref[slabs + n, step_rows, :] = s_im[:, n * LANES:(n + 1) * LANES]
        n_re = are * s_re - aim * s_im + v_re
        n_im = are * s_im + aim * s_re + v_im
        s_re, s_im = n_re, n_im
    sre_ref[...] = s_re
    sim_ref[...] = s_im

    spb = jnp.concatenate(
        [jnp.concatenate([sp_ref[n, b * pitch:b * pitch + c, :] for b in range(nb)], axis=0)
         for n in range(2 * slabs)], axis=1).astype(BF16)
    d = d_ref[0]
    steps_per_tile = mxu // LANES
    for n in range(q * LANES // mxu):
        yn = yi_ref[n] + _dot(spb, wo_ref[:, n * mxu:(n + 1) * mxu])
        for s in range(steps_per_tile):
            j = n * steps_per_tile + s
            yj = yn[:, s * LANES:(s + 1) * LANES] + d * us[j]
            y_ref[:, pl.ds(j, c, stride=q), :] = yj.reshape(nb, c, LANES)


def _s5_branch(u3, kd, sbd, cod, are, aim, d):
    nb, seq, width = u3.shape
    nt = width // LANES
    tt = S5_TT
    kq = S5_Q * LANES
    half = are.shape[-1]
    r = nb * (tt // S5_Q + S5_ROW_PAD)
    tile = pl.BlockSpec((nb, tt, LANES), lambda l, t: (0, t, l))
    per_tile = lambda arr: pl.BlockSpec((1,) + arr.shape[1:], lambda l, t: (l, 0, 0))
    return pl.pallas_call(
        _s5_kernel,
        grid=(nt, seq // tt),
        in_specs=[tile] + [per_tile(arr) for arr in (kd, sbd, cod, are, aim, d)],
        out_specs=tile,
        out_shape=jax.ShapeDtypeStruct(u3.shape, F32),
        scratch_shapes=[pltpu.VMEM((nb, half), F32), pltpu.VMEM((nb, half), F32),
                        pltpu.VMEM((2 * half // LANES, r, LANES), F32),
                        pltpu.VMEM((2 * half // LANES, r, LANES), F32),
                        pltpu.VMEM((kq // (2 * LANES), nb * tt // S5_Q, 2 * LANES), F32),
                        pltpu.VMEM((kq, kq), BF16), pltpu.VMEM((kq, 2 * half), BF16),
                        pltpu.VMEM((2 * half, kq), BF16)],
        compiler_params=pltpu.CompilerParams(
            dimension_semantics=("arbitrary", "arbitrary"), vmem_limit_bytes=VMEM_LIMIT),
        name="s5_branch",
    )(u3, kd, sbd, cod, are, aim, d)


def _split3(x):
    hi = x.astype(BF16)
    r1 = x - hi.astype(F32)
    mid = r1.astype(BF16)
    lo = (r1 - mid.astype(F32)).astype(BF16)
    return hi, mid, lo


def _project_block(hb_ref, w_ref, dst_ref, src_col, dst_col, n):
    dst_ref[:, dst_col:dst_col + n] = _dot(hb_ref[...], w_ref[:, src_col:src_col + n])


def _proj_ssd_kernel(x_ref, g_ref, w_ref, wg_ref, cw_ref, cb_ref, dtb_ref, a2_ref, dskip_ref,
                     ng_ref, u_ref, ga_ref, gb_ref, y_ref,
                     hb_ref, xc_ref, z_ref, acum_ref, acumt_ref, dtt_ref, wt_ref,
                     ys_ref, st_ref, *pad_refs):
    tt, width = y_ref.shape
    q = SSD_Q
    n_state = SSD_STATE
    slabs = len(pad_refs)
    pairs = width // LANES
    groups = (slabs - pairs) // 2
    halo = SUBLANES
    taps = cw_ref.shape[0]
    mxu = 2 * LANES
    o_z = u_ref.shape[-1]
    o_xbc = o_z + width
    o_dt = o_xbc + slabs * LANES

    @pl.when(pl.program_id(1) == 0)
    def _():
        for pad_ref in pad_refs:
            pad_ref[tt:tt + halo, :] = jnp.zeros((halo, LANES), F32)
        st_ref[...] = jnp.zeros_like(st_ref)

    hb_ref[...] = _rms(x_ref[...], g_ref[...]).astype(BF16)

    def project(src_col, n):
        return _dot(hb_ref[...], w_ref[:, src_col:src_col + n])

    row_i = lax.broadcasted_iota(jnp.int32, (q, q), 0)
    col_i = lax.broadcasted_iota(jnp.int32, (q, q), 1)
    causal = row_i >= col_i
    tri = causal.astype(BF16)
    lane = lax.broadcasted_iota(jnp.int32, (q, LANES), 1)
    first_head = lane < SSD_HEADDIM
    first_mask = first_head.astype(BF16)
    second_mask = 1.0 - first_mask

    v = project(o_dt, LANES) + dtb_ref[...]
    dts = jnp.maximum(v, 0.0) + jnp.log1p(jnp.exp(-jnp.abs(v)))
    n_chunks = tt // q

    def decay_tables():
        for ci in range(n_chunks):
            rows = slice(ci * q, (ci + 1) * q)
            dtq = dts[rows]
            hi, mid, lo = _split3(dtq * a2_ref[...])
            parts = _dot(tri, jnp.concatenate([hi, mid, lo], axis=1))
            acum = (parts[:, :LANES] + parts[:, LANES:2 * LANES]) + parts[:, 2 * LANES:]
            acum_t = acum.T
            dt_t = dtq.T
            acum_ref[rows, :] = acum
            acumt_ref[ci] = acum_t
            dtt_ref[ci] = dt_t
            wt_ref[ci] = jnp.exp2(acum_t[:, q - 1:q] - acum_t) * dt_t

    def deferred(dst_ref, src_ref, src_col):
        return [functools.partial(_project_block, hb_ref, src_ref, dst_ref, src_col + c, c, mxu)
                for c in range(0, dst_ref.shape[-1], mxu)]

    z_blocks = deferred(z_ref, w_ref, o_z)
    other_blocks = (deferred(u_ref, w_ref, 0) + deferred(ga_ref, wg_ref, 0)
                    + deferred(gb_ref, wg_ref, ga_ref.shape[-1]))
    xbc_blocks = slabs * LANES // mxu
    assert len(z_blocks) <= xbc_blocks

    half = tt // 2
    for blk in range(xbc_blocks):
        res = project(o_xbc + blk * mxu, mxu)
        for i in range(mxu // LANES):
            n = blk * (mxu // LANES) + i
            cols = slice(n * LANES, (n + 1) * LANES)
            pad_ref = pad_refs[n]
            pad_ref[0:halo, :] = pad_ref[tt:tt + halo, :]
            pad_ref[halo:, :] = res[:, i * LANES:(i + 1) * LANES]
            for s in range(2):
                acc = cb_ref[:, cols]
                for k in range(taps):
                    start = halo - (taps - 1) + s + k
                    acc = acc + cw_ref[k:k + 1, cols] * pad_ref[pl.ds(start, half, stride=2), :]
                xc_ref[n, pl.ds(s, half, stride=2), :] = acc + acc * jnp.tanh(acc)
        if blk == TABLES_AFTER_BLOCK:
            decay_tables()
        if blk >= xbc_blocks - len(z_blocks):
            z_blocks[blk - (xbc_blocks - len(z_blocks))]()

    slots = n_chunks * pairs
    issue_at = {(i * slots) // len(other_blocks): blk for i, blk in enumerate(other_blocks)}
    assert len(issue_at) == len(other_blocks)

    for ci in range(n_chunks):
        rows = slice(ci * q, (ci + 1) * q)
        acum = acum_ref[rows, :]
        acum_t = acumt_ref[ci]
        dt_t = dtt_ref[ci]
        w_t = wt_ref[ci]
        for g in range(groups):
            bg = xc_ref[pairs + g, rows, :]
            cg = xc_ref[pairs + groups + g, rows, :].astype(BF16)
            cb = lax.dot_general(cg, bg.astype(BF16), (((1,), (1,)), ((), ())),
                                 preferred_element_type=F32)
            cbm = jnp.where(causal, cb, 0.0).astype(BF16)
            bgt = bg.T.astype(BF16)
            st_prevs = [st_ref[2 * g], st_ref[2 * g + 1]]
            y_offs = _dot(cg, jnp.concatenate([s.astype(BF16) for s in st_prevs], axis=1))
            for m in (2 * g, 2 * g + 1):
                if ci * pairs + m in issue_at:
                    issue_at[ci * pairs + m]()
                xs = xc_ref[m, rows, :]
                mats, wts, cols = [], [], []
                for h in (2 * m, 2 * m + 1):
                    col = jnp.broadcast_to(acum[:, h:h + 1], (q, q))
                    cols.append(col)
                    rowv = jnp.broadcast_to(acum_t[h:h + 1, :], (q, q))
                    dtrow = jnp.broadcast_to(dt_t[h:h + 1, :], (q, q)).astype(BF16)
                    decay = jnp.exp2(jnp.minimum(col - rowv, 0.0)).astype(BF16)
                    mats.append(cbm * decay * dtrow)
                    wts.append(bgt * jnp.broadcast_to(w_t[h:h + 1, :], (n_state, q)).astype(BF16))
                mcat = jnp.concatenate(mats, axis=1)
                wcat = jnp.concatenate(wts, axis=1)
                xsb = xs.astype(BF16)
                xbd = jnp.concatenate([xsb * first_mask, xsb * second_mask], axis=0)
                y_diag = _dot(mcat, xbd)
                st_new = _dot(wcat, xbd)
                st_prev = st_prevs[m - 2 * g]
                ecol = jnp.exp2(jnp.where(first_head, cols[0], cols[1]))
                y_off = y_offs[:, (m - 2 * g) * LANES:(m - 2 * g + 1) * LANES] * ecol
                st_ref[m] = st_prev * ecol[q - 1:q, :] + st_new
                ys_ref[rows, m * LANES:(m + 1) * LANES] = (
                    y_diag + y_off + xs * dskip_ref[:, m * LANES:(m + 1) * LANES])
        hz = 0.5 * z_ref[rows, :]
        yg = ys_ref[rows, :] * (hz + hz * jnp.tanh(hz))
        y_ref[rows, :] = _rms(yg, ng_ref[...])


def _proj_ssd(x3, g, w, wg, cw, cb, dtb, a2, dskip, ng, s5_width, d_model):
    nb, seq, d_in = x3.shape
    width = ng.shape[-1]
    cdim = cw.shape[-1]
    tt = SSD_TT
    slabs = cdim // LANES
    chunks = tt // SSD_Q
    tile = lambda n: pl.BlockSpec((None, tt, n), lambda b, t: (b, t, 0))
    const = lambda arr: pl.BlockSpec(arr.shape, lambda b, t: (0, 0), pipeline_mode=pl.Buffered(1))
    out_widths = (s5_width, d_model, d_model, width)
    return pl.pallas_call(
        _proj_ssd_kernel,
        grid=(nb, seq // tt),
        in_specs=[tile(d_in)] + [const(arr) for arr in (g, w, wg, cw, cb, dtb, a2, dskip, ng)],
        out_specs=[tile(n) for n in out_widths],
        out_shape=[jax.ShapeDtypeStruct((nb, seq, n), F32) for n in out_widths],
        scratch_shapes=[pltpu.VMEM((tt, d_in), BF16),
                        pltpu.VMEM((slabs, tt, LANES), F32),
                        pltpu.VMEM((tt, width), F32),
                        pltpu.VMEM((tt, LANES), F32),
                        pltpu.VMEM((chunks, SSD_Q, SSD_Q), F32),
                        pltpu.VMEM((chunks, SSD_Q, SSD_Q), F32),
                        pltpu.VMEM((chunks, SSD_Q, SSD_Q), F32),
                        pltpu.VMEM((tt, width), F32),
                        pltpu.VMEM((width // LANES, SSD_STATE, LANES), F32)]
        + [pltpu.VMEM((tt + SUBLANES, LANES), F32) for _ in range(slabs)],
        compiler_params=pltpu.CompilerParams(
            dimension_semantics=("arbitrary", "arbitrary"), vmem_limit_bytes=VMEM_LIMIT),
        name="proj_ssd",
    )(x3, g, w, wg, cw, cb, dtb, a2, dskip, ng)


def _gelu_tanh(x):
    return 0.5 * x * (1.0 + jnp.tanh(math.sqrt(2.0 / math.pi) * (x + 0.044715 * (x * x * x))))


def _tail_kernel(x_ref, y5_ref, yb_ref, ga_ref, gb_ref, gluw_ref, glub_ref, wa_ref, wb_ref,
                 wo_ref, nm_ref, w1_ref, w2_ref, nf_ref, o_ref, *, apply_final):
    ya = _gelu_tanh(y5_ref[...])
    ya = ya * _sigmoid(_dot(ya.astype(BF16), gluw_ref[...]) + glub_ref[...])
    ya = _dot(ya.astype(BF16), wa_ref[...])
    yb = _dot(yb_ref[...].astype(BF16), wb_ref[...])
    merged = _sigmoid(ga_ref[...]) * ya + _sigmoid(gb_ref[...]) * yb
    x1 = x_ref[...] + _dot(merged.astype(BF16), wo_ref[...])
    hb = _rms(x1, nm_ref[...]).astype(BF16)
    acc = x1
    d_ff = w1_ref.shape[-1]
    ff_block = 1024
    for k in range(d_ff // ff_block):
        cols = slice(k * ff_block, (k + 1) * ff_block)
        act = jnp.maximum(_dot(hb, w1_ref[:, cols]), 0.0)
        acc = acc + _dot((act * act).astype(BF16), w2_ref[cols, :])
    if apply_final:
        acc = _rms(acc, nf_ref[...])
    o_ref[...] = acc


def _tail(x2, y5, yb, ga, gb, gluw, glub, wa, wb, wo, nm, w1, w2, nf, apply_final):
    t, d = x2.shape
    tm = TOKEN_TILE
    row = lambda n: pl.BlockSpec((tm, n), lambda i: (i, 0))
    const = lambda arr: pl.BlockSpec(arr.shape, lambda i: (0, 0), pipeline_mode=pl.Buffered(1))
    consts = (gluw, glub, wa, wb, wo, nm, w1, w2, nf)
    return pl.pallas_call(
        functools.partial(_tail_kernel, apply_final=apply_final),
        grid=(t // tm,),
        in_specs=[row(d), row(y5.shape[-1]), row(yb.shape[-1]), row(d), row(d)]
        + [const(c) for c in consts],
        out_specs=row(d),
        out_shape=jax.ShapeDtypeStruct((t, d), F32),
        compiler_params=pltpu.CompilerParams(
            dimension_semantics=("arbitrary",), vmem_limit_bytes=VMEM_LIMIT),
        name="tail",
    )(x2, y5, yb, ga, gb, *consts)


def kernel(x, norm_mix, w_in, s5_a_re, s5_a_im, s5_log_dt, s5_b_re, s5_b_im, s5_c_re, s5_c_im,
           s5_d, s5_glu_w, s5_glu_b, w_branch_a, ssd_conv_w, ssd_conv_b, ssd_dt_bias, ssd_a_log,
           ssd_d, ssd_norm, w_branch_b, w_out, norm_mlp, w_ff1, w_ff2, norm_final):
    bsz, seq, d_model = x.shape
    depth = w_in.shape[0]
    s5_width = s5_glu_w.shape[-1]
    ssd_width = w_branch_b.shape[1]
    conv_dim = ssd_conv_w.shape[-1]
    heads = ssd_a_log.shape[-1]
    o_dt = s5_width + ssd_width + conv_dim
    o_ga = o_dt + heads
    assert o_dt % LANES == 0 and heads <= LANES and o_dt + LANES <= w_in.shape[-1]
    row = lambda v: v.reshape(1, -1).astype(F32)
    pad_lanes = lambda v: jnp.pad(v.reshape(1, -1).astype(F32), ((0, 0), (0, LANES - v.shape[-1])))

    x2 = x.reshape(bsz * seq, d_model)
    for i in range(depth):
        u, ga, gb, yb = _proj_ssd(
            x2.reshape(bsz, seq, d_model), row(norm_mix[i]),
            w_in[i].astype(BF16), w_in[i, :, o_ga:].astype(BF16),
            0.5 * ssd_conv_w[i], row(0.5 * ssd_conv_b[i]),
            pad_lanes(ssd_dt_bias[i]), pad_lanes(-jnp.exp(ssd_a_log[i]) * LOG2E),
            row(jnp.repeat(ssd_d[i], SSD_HEADDIM)), row(ssd_norm[i]), s5_width, d_model)

        kd, sbd, cod, are, aim = _s5_operators(s5_a_re[i], s5_a_im[i], s5_log_dt[i], s5_b_re[i],
                                               s5_b_im[i], s5_c_re[i], s5_c_im[i])
        y5 = _s5_branch(u, kd, sbd, cod, are, aim, s5_d[i].reshape(s5_width // LANES, 1, LANES))

        x2 = _tail(x2, y5.reshape(bsz * seq, s5_width), yb.reshape(bsz * seq, ssd_width),
                   ga.reshape(bsz * seq, d_model), gb.reshape(bsz * seq, d_model),
                   s5_glu_w[i].astype(BF16), row(s5_glu_b[i]), w_branch_a[i].astype(BF16),
                   w_branch_b[i].astype(BF16), w_out[i].astype(BF16), row(norm_mlp[i]),
                   w_ff1[i].astype(BF16), w_ff2[i].astype(BF16), row(norm_final),
                   apply_final=(i == depth - 1))
    return x2.reshape(bsz, seq, d_model)
```

```python
import functools
import math

import jax
import jax.numpy as jnp
from jax import lax
from jax.experimental import pallas as pl
from jax.experimental.pallas import tpu as pltpu

F32 = jnp.float32
BF16 = jnp.bfloat16

EPS = 1e-6
LANES = 128
SUBLANES = 8

S5_GROUP = 16
S5_STATE = 64
S5_Q = 8
S5_GROUPS_PER_TILE = LANES // S5_GROUP
S5_TT = 512
S5_ROW_PAD = SUBLANES

SSD_HEADDIM = 64
SSD_STATE = 128
SSD_CONV = 4
SSD_Q = 128
SSD_TT = 512
TABLES_AFTER_BLOCK = 2
LOG2E = math.log2(math.e)

TOKEN_TILE = 512
VMEM_LIMIT = 56 * 1024 * 1024


def _rms(x, g):
    return x * lax.rsqrt(jnp.mean(x * x, axis=-1, keepdims=True) + EPS) * g


def _sigmoid(x):
    return 1.0 / (1.0 + jnp.exp(-x))


def _dot(a, b):
    return jnp.dot(a, b, preferred_element_type=F32)


def _s5_operators(a_re, a_im, log_dt, b_re, b_im, c_re, c_im):
    q = S5_Q
    g, p = a_re.shape
    dt = jnp.exp(log_dt)[:, None]
    steps = jnp.arange(q + 1, dtype=F32)[:, None, None]
    mag = jnp.exp(a_re * dt * steps)
    ang = a_im * dt * steps
    pw_re = mag * jnp.cos(ang)
    pw_im = mag * jnp.sin(ang)
    nr = pw_re[1] - 1.0
    ni = pw_im[1]
    den = a_re * a_re + a_im * a_im
    cr = (nr * a_re + ni * a_im) / den
    ci = (ni * a_re - nr * a_im) / den
    nt = g // S5_GROUPS_PER_TILE
    g8 = S5_GROUPS_PER_TILE
    hh = S5_GROUP
    kq = q * LANES
    bt_re = b_re.transpose(0, 2, 1)
    bt_im = b_im.transpose(0, 2, 1)
    bb_re = cr[:, None, :] * bt_re - ci[:, None, :] * bt_im
    bb_im = cr[:, None, :] * bt_im + ci[:, None, :] * bt_re
    cp_re = c_re[None] * pw_re[:, :, None, :] - c_im[None] * pw_im[:, :, None, :]
    cp_im = c_re[None] * pw_im[:, :, None, :] + c_im[None] * pw_re[:, :, None, :]
    kj = jnp.sum(cp_re[:q, :, :, None, :] * bb_re[None, :, None, :, :]
                 - cp_im[:q, :, :, None, :] * bb_im[None, :, None, :, :], axis=-1)
    kd = kj.transpose(1, 3, 0, 2).reshape(nt, LANES, q * hh)
    rev_steps = (q - 1) - steps[:q]
    rev_mag = jnp.exp(a_re * dt * rev_steps)
    rev_ang = a_im * dt * rev_steps
    rev_re = (rev_mag * jnp.cos(rev_ang))[:, :, None, :]
    rev_im = (rev_mag * jnp.sin(rev_ang))[:, :, None, :]
    sb = jnp.concatenate([rev_re * bb_re[None] - rev_im * bb_im[None],
                          rev_re * bb_im[None] + rev_im * bb_re[None]], axis=-1)
    sbd = sb.reshape(q, nt, LANES, 2 * p).transpose(1, 0, 2, 3).reshape(nt, kq, 2 * p)
    ct_re = c_re.transpose(0, 2, 1)
    ct_im = c_im.transpose(0, 2, 1)
    co_re = jnp.concatenate([ct_re * pw_re[t + 1][:, :, None] - ct_im * pw_im[t + 1][:, :, None]
                             for t in range(q)], axis=-1)
    co_im = jnp.concatenate([-(ct_re * pw_im[t + 1][:, :, None] + ct_im * pw_re[t + 1][:, :, None])
                             for t in range(q)], axis=-1)
    cod = (jnp.stack([co_re, co_im], axis=0).reshape(2, nt, g8 * p, q * hh)
           .transpose(1, 0, 2, 3).reshape(nt, 2 * g8 * p, q * hh))
    are = pw_re[q].reshape(nt, 1, g8 * p)
    aim = pw_im[q].reshape(nt, 1, g8 * p)
    return kd.astype(BF16), sbd.astype(BF16), cod.astype(BF16), are, aim


def _field(idx, period, width):
    return (idx & (period - 1)) >> (width.bit_length() - 1)


def _s5_expand(compact, rep, row_grp, col_grp):
    wide = _dot(compact, rep)
    return jnp.where(row_grp == col_grp, wide, 0.0).astype(BF16)


def _s5_build_operators(kd_ref, sbd_ref, cod_ref, wi_ref, ws_ref, wo_ref):
    q = S5_Q
    hh = S5_GROUP
    gw = LANES // 2
    cw = q * gw
    p = sbd_ref.shape[-1] // 2
    sw = wo_ref.shape[1]
    iota = lambda shape, d: lax.broadcasted_iota(jnp.int32, shape, d)

    def rep_th(shift):
        r, c = iota((q * hh, cw), 0), iota((q * hh, cw), 1)
        return ((_field(r, q * hh, hh) + shift == _field(c, cw, gw))
                & (_field(r, hh, 1) == _field(c, hh, 1))).astype(BF16)

    r2, c2 = iota((2 * p, sw), 0), iota((2 * p, sw), 1)
    rep_sp = ((_field(r2, 2 * p, p) == _field(c2, sw, sw // 2))
              & (_field(r2, p, 1) == _field(c2, p, 1))).astype(BF16)
    chan_grp = lambda idx: _field(idx, gw, hh)
    state_grp = lambda idx: _field(idx, sw // 2, p)
    for hf in range(2):
        for j in range(q):
            wi_ref[hf, j * gw:(j + 1) * gw, :] = _s5_expand(
                kd_ref[0, hf * gw:(hf + 1) * gw, :], rep_th(j),
                chan_grp(iota((gw, cw), 0)), chan_grp(iota((gw, cw), 1)))
        sb_half = jnp.concatenate(
            [sbd_ref[0, j * LANES + hf * gw:j * LANES + (hf + 1) * gw, :] for j in range(q)], axis=0)
        ws_ref[hf] = _s5_expand(sb_half, rep_sp, chan_grp(iota((cw, sw), 0)), state_grp(iota((cw, sw), 1)))
        co_half = jnp.concatenate(
            [cod_ref[0, s * sw + hf * (sw // 2):s * sw + (hf + 1) * (sw // 2), :] for s in range(2)], axis=0)
        wo_ref[hf] = _s5_expand(co_half, rep_th(0), state_grp(iota((sw, cw), 0)), chan_grp(iota((sw, cw), 1)))


def _s5_kernel(u_ref, kd_ref, sbd_ref, cod_ref, are_ref, aim_ref, d_ref, y_ref,
               sre_ref, sim_ref, v_ref, sp_ref, yi_ref, wi_ref, ws_ref, wo_ref):
    q = S5_Q
    nb, tt, _ = u_ref.shape
    c = tt // q
    r = nb * c
    half = sre_ref.shape[-1]

    @pl.when(pl.program_id(1) == 0)
    def _():
        sre_ref[...] = jnp.zeros_like(sre_ref)
        sim_ref[...] = jnp.zeros_like(sim_ref)
        _s5_build_operators(kd_ref, sbd_ref, cod_ref, wi_ref, ws_ref, wo_ref)

    us = [u_ref[:, pl.ds(j, c, stride=q), :].reshape(r, LANES) for j in range(q)]

    low = lax.broadcasted_iota(jnp.int32, (r, LANES), 1) < LANES // 2
    swap = lambda x: pltpu.roll(x, LANES // 2, 1)
    halves = ([], [])
    for pair in range(q // 2):
        even, odd = us[2 * pair], us[2 * pair + 1]
        halves[0].append(jnp.where(low, even, swap(odd)).astype(BF16))
        halves[1].append(jnp.where(low, swap(even), odd).astype(BF16))
    a = [jnp.concatenate(slabs_h, axis=1) for slabs_h in halves]

    slabs = half // LANES
    per_half = slabs // 2
    pitch = v_ref.shape[1] // nb
    mxu = 2 * LANES
    for hf in range(2):
        v = _dot(a[hf], ws_ref[hf])
        for part in range(2):
            for i in range(per_half):
                cols = slice((part * per_half + i) * LANES, (part * per_half + i + 1) * LANES)
                n = part * slabs + hf * per_half + i
                for b in range(nb):
                    v_ref[n, b * pitch:b * pitch + c, :] = v[b * c:(b + 1) * c, cols]

    tiles_per_half = a[0].shape[-1] // mxu
    for hf in range(2):
        for n in range(tiles_per_half):
            yi_ref[hf * tiles_per_half + n] = _dot(
                a[hf][:, :(n + 1) * mxu], wi_ref[hf, :(n + 1) * mxu, n * mxu:(n + 1) * mxu])

    are = jnp.broadcast_to(are_ref[0], (nb, half))
    aim = jnp.broadcast_to(aim_ref[0], (nb, half))
    s_re = sre_ref[...]
    s_im = sim_ref[...]
    for k in range(c):
        step_rows = pl.ds(k, nb, stride=pitch)
        v_re = jnp.concatenate([v_ref[n, step_rows, :] for n in range(slabs)], axis=1)
        v_im = jnp.concatenate([v_ref[slabs + n, step_rows, :] for n in range(slabs)], axis=1)
        for n in range(slabs):
            sp_ref[n, step_rows, :] = s_re[:, n * LANES:(n + 1) * LANES]
            sp_ref[slabs + n, step_rows, :] = s_im[:, n * LANES:(n + 1) * LANES]
        n_re = are * s_re - aim * s_im + v_re
        n_im = are * s_im + aim * s_re + v_im
        s_re, s_im = n_re, n_im
    sre_ref[...] = s_re
    sim_ref[...] = s_im

    def entering_states(hf):
        ns = [part * slabs + hf * per_half + i for part in range(2) for i in range(per_half)]
        return jnp.concatenate(
            [jnp.concatenate([sp_ref[n, b * pitch:b * pitch + c, :] for b in range(nb)], axis=0)
             for n in ns], axis=1).astype(BF16)

    y_half = []
    for hf in range(2):
        spb = entering_states(hf)
        y_half.append(jnp.concatenate(
            [yi_ref[hf * tiles_per_half + n] + _dot(spb, wo_ref[hf, :, n * mxu:(n + 1) * mxu])
             for n in range(tiles_per_half)], axis=1))
    d = d_ref[0]
    for pair in range(q // 2):
        y0 = y_half[0][:, pair * LANES:(pair + 1) * LANES]
        y1 = y_half[1][:, pair * LANES:(pair + 1) * LANES]
        steps = ((2 * pair, jnp.where(low, y0, swap(y1))), (2 * pair + 1, jnp.where(low, swap(y0), y1)))
        for j, yj in steps:
            y_ref[:, pl.ds(j, c, stride=q), :] = (yj + d * us[j]).reshape(nb, c, LANES)


def _s5_branch(u3, kd, sbd, cod, are, aim, d):
    nb, seq, width = u3.shape
    nt = width // LANES
    tt = S5_TT
    kq = S5_Q * LANES
    half = are.shape[-1]
    r = nb * (tt // S5_Q + S5_ROW_PAD)
    tile = pl.BlockSpec((nb, tt, LANES), lambda l, t: (0, t, l))
    per_tile = lambda arr: pl.BlockSpec((1,) + arr.shape[1:], lambda l, t: (l, 0, 0))
    return pl.pallas_call(
        _s5_kernel,
        grid=(nt, seq // tt),
        in_specs=[tile] + [per_tile(arr) for arr in (kd, sbd, cod, are, aim, d)],
        out_specs=tile,
        out_shape=jax.ShapeDtypeStruct(u3.shape, F32),
        scratch_shapes=[pltpu.VMEM((nb, half), F32), pltpu.VMEM((nb, half), F32),
                        pltpu.VMEM((2 * half // LANES, r, LANES), F32),
                        pltpu.VMEM((2 * half // LANES, r, LANES), F32),
                        pltpu.VMEM((kq // (2 * LANES), nb * tt // S5_Q, 2 * LANES), F32),
                        pltpu.VMEM((2, kq // 2, kq // 2), BF16), pltpu.VMEM((2, kq // 2, half), BF16),
                        pltpu.VMEM((2, half, kq // 2), BF16)],
        compiler_params=pltpu.CompilerParams(
            dimension_semantics=("arbitrary", "arbitrary"), vmem_limit_bytes=VMEM_LIMIT),
        name="s5_branch",
    )(u3, kd, sbd, cod, are, aim, d)


def _split3(x):
    hi = x.astype(BF16)
    r1 = x - hi.astype(F32)
    mid = r1.astype(BF16)
    lo = (r1 - mid.astype(F32)).astype(BF16)
    return hi, mid, lo


def _project_block(hb_ref, w_ref, dst_ref, src_col, dst_col, n):
    dst_ref[:, dst_col:dst_col + n] = _dot(hb_ref[...], w_ref[:, src_col:src_col + n])


def _proj_ssd_kernel(x_ref, g_ref, w_ref, wg_ref, cw_ref, cb_ref, dtb_ref, a2_ref, dskip_ref,
                     ng_ref, u_ref, ga_ref, gb_ref, y_ref,
                     hb_ref, xc_ref, z_ref, acum_ref, acumt_ref, dtt_ref, wt_ref,
                     ys_ref, st_ref, *pad_refs):
    tt, width = y_ref.shape
    q = SSD_Q
    n_state = SSD_STATE
    slabs = len(pad_refs)
    pairs = width // LANES
    groups = (slabs - pairs) // 2
    halo = SUBLANES
    taps = cw_ref.shape[0]
    mxu = 2 * LANES
    o_z = u_ref.shape[-1]
    o_xbc = o_z + width
    o_dt = o_xbc + slabs * LANES

    @pl.when(pl.program_id(1) == 0)
    def _():
        for pad_ref in pad_refs:
            pad_ref[tt:tt + halo, :] = jnp.zeros((halo, LANES), F32)
        st_ref[...] = jnp.zeros_like(st_ref)

    hb_ref[...] = _rms(x_ref[...], g_ref[...]).astype(BF16)

    def project(src_col, n):
        return _dot(hb_ref[...], w_ref[:, src_col:src_col + n])

    row_i = lax.broadcasted_iota(jnp.int32, (q, q), 0)
    col_i = lax.broadcasted_iota(jnp.int32, (q, q), 1)
    causal = row_i >= col_i
    tri = causal.astype(BF16)
    lane = lax.broadcasted_iota(jnp.int32, (q, LANES), 1)
    first_head = lane < SSD_HEADDIM
    first_mask = first_head.astype(BF16)
    second_mask = 1.0 - first_mask

    v = project(o_dt, LANES) + dtb_ref[...]
    dts = jnp.maximum(v, 0.0) + jnp.log1p(jnp.exp(-jnp.abs(v)))
    n_chunks = tt // q

    def decay_tables():
        for ci in range(n_chunks):
            rows = slice(ci * q, (ci + 1) * q)
            dtq = dts[rows]
            hi, mid, lo = _split3(dtq * a2_ref[...])
            parts = _dot(tri, jnp.concatenate([hi, mid, lo], axis=1))
            acum = (parts[:, :LANES] + parts[:, LANES:2 * LANES]) + parts[:, 2 * LANES:]
            acum_t = acum.T
            dt_t = dtq.T
            acum_ref[rows, :] = acum
            acumt_ref[ci] = acum_t
            dtt_ref[ci] = dt_t
            wt_ref[ci] = jnp.exp2(acum_t[:, q - 1:q] - acum_t) * dt_t

    def deferred(dst_ref, src_ref, src_col):
        return [functools.partial(_project_block, hb_ref, src_ref, dst_ref, src_col + c, c, mxu)
                for c in range(0, dst_ref.shape[-1], mxu)]

    z_blocks = deferred(z_ref, w_ref, o_z)
    other_blocks = (deferred(u_ref, w_ref, 0) + deferred(ga_ref, wg_ref, 0)
                    + deferred(gb_ref, wg_ref, ga_ref.shape[-1]))
    xbc_blocks = slabs * LANES // mxu
    assert len(z_blocks) <= xbc_blocks

    half = tt // 2
    for blk in range(xbc_blocks):
        res = project(o_xbc + blk * mxu, mxu)
        for i in range(mxu // LANES):
            n = blk * (mxu // LANES) + i
            cols = slice(n * LANES, (n + 1) * LANES)
            pad_ref = pad_refs[n]
            pad_ref[0:halo, :] = pad_ref[tt:tt + halo, :]
            pad_ref[halo:, :] = res[:, i * LANES:(i + 1) * LANES]
            for s in range(2):
                acc = cb_ref[:, cols]
                for k in range(taps):
                    start = halo - (taps - 1) + s + k
                    acc = acc + cw_ref[k:k + 1, cols] * pad_ref[pl.ds(start, half, stride=2), :]
                xc_ref[n, pl.ds(s, half, stride=2), :] = acc + acc * jnp.tanh(acc)
        if blk == TABLES_AFTER_BLOCK:
            decay_tables()
        if blk >= xbc_blocks - len(z_blocks):
            z_blocks[blk - (xbc_blocks - len(z_blocks))]()

    slots = n_chunks * pairs
    issue_at = {(i * slots) // len(other_blocks): blk for i, blk in enumerate(other_blocks)}
    assert len(issue_at) == len(other_blocks)

    for ci in range(n_chunks):
        rows = slice(ci * q, (ci + 1) * q)
        acum = acum_ref[rows, :]
        acum_t = acumt_ref[ci]
        dt_t = dtt_ref[ci]
        w_t = wt_ref[ci]
        for g in range(groups):
            bg = xc_ref[pairs + g, rows, :]
            cg = xc_ref[pairs + groups + g, rows, :].astype(BF16)
            cb = lax.dot_general(cg, bg.astype(BF16), (((1,), (1,)), ((), ())),
                                 preferred_element_type=F32)
            cbm = jnp.where(causal, cb, 0.0).astype(BF16)
            bgt = bg.T.astype(BF16)
            st_prevs = [st_ref[2 * g], st_ref[2 * g + 1]]
            y_offs = _dot(cg, jnp.concatenate([s.astype(BF16) for s in st_prevs], axis=1))
            for m in (2 * g, 2 * g + 1):
                if ci * pairs + m in issue_at:
                    issue_at[ci * pairs + m]()
                xs = xc_ref[m, rows, :]
                mats, wts, cols = [], [], []
                for h in (2 * m, 2 * m + 1):
                    col = jnp.broadcast_to(acum[:, h:h + 1], (q, q))
                    cols.append(col)
                    rowv = jnp.broadcast_to(acum_t[h:h + 1, :], (q, q))
                    dtrow = jnp.broadcast_to(dt_t[h:h + 1, :], (q, q)).astype(BF16)
                    decay = jnp.exp2(jnp.minimum(col - rowv, 0.0)).astype(BF16)
                    mats.append(cbm * decay * dtrow)
                    wts.append(bgt * jnp.broadcast_to(w_t[h:h + 1, :], (n_state, q)).astype(BF16))
                mcat = jnp.concatenate(mats, axis=1)
                wcat = jnp.concatenate(wts, axis=1)
                xsb = xs.astype(BF16)
                xbd = jnp.concatenate([xsb * first_mask, xsb * second_mask], axis=0)
                y_diag = _dot(mcat, xbd)
                st_new = _dot(wcat, xbd)
                st_prev = st_prevs[m - 2 * g]
                ecol = jnp.exp2(jnp.where(first_head, cols[0], cols[1]))
                y_off = y_offs[:, (m - 2 * g) * LANES:(m - 2 * g + 1) * LANES] * ecol
                st_ref[m] = st_prev * ecol[q - 1:q, :] + st_new
                ys_ref[rows, m * LANES:(m + 1) * LANES] = (
                    y_diag + y_off + xs * dskip_ref[:, m * LANES:(m + 1) * LANES])
        hz = 0.5 * z_ref[rows, :]
        yg = ys_ref[rows, :] * (hz + hz * jnp.tanh(hz))
        y_ref[rows, :] = _rms(yg, ng_ref[...])


def _proj_ssd(x3, g, w, wg, cw, cb, dtb, a2, dskip, ng, s5_width, d_model):
    nb, seq, d_in = x3.shape
    width = ng.shape[-1]
    cdim = cw.shape[-1]
    tt = SSD_TT
    slabs = cdim // LANES
    chunks = tt // SSD_Q
    tile = lambda n: pl.BlockSpec((None, tt, n), lambda b, t: (b, t, 0))
    const = lambda arr: pl.BlockSpec(arr.shape, lambda b, t: (0, 0), pipeline_mode=pl.Buffered(1))
    out_widths = (s5_width, d_model, d_model, width)
    return pl.pallas_call(
        _proj_ssd_kernel,
        grid=(nb, seq // tt),
        in_specs=[tile(d_in)] + [const(arr) for arr in (g, w, wg, cw, cb, dtb, a2, dskip, ng)],
        out_specs=[tile(n) for n in out_widths],
        out_shape=[jax.ShapeDtypeStruct((nb, seq, n), F32) for n in out_widths],
        scratch_shapes=[pltpu.VMEM((tt, d_in), BF16),
                        pltpu.VMEM((slabs, tt, LANES), F32),
                        pltpu.VMEM((tt, width), F32),
                        pltpu.VMEM((tt, LANES), F32),
                        pltpu.VMEM((chunks, SSD_Q, SSD_Q), F32),
                        pltpu.VMEM((chunks, SSD_Q, SSD_Q), F32),
                        pltpu.VMEM((chunks, SSD_Q, SSD_Q), F32),
                        pltpu.VMEM((tt, width), F32),
                        pltpu.VMEM((width // LANES, SSD_STATE, LANES), F32)]
        + [pltpu.VMEM((tt + SUBLANES, LANES), F32) for _ in range(slabs)],
        compiler_params=pltpu.CompilerParams(
            dimension_semantics=("arbitrary", "arbitrary"), vmem_limit_bytes=VMEM_LIMIT),
        name="proj_ssd",
    )(x3, g, w, wg, cw, cb, dtb, a2, dskip, ng)


def _gelu_tanh(x):
    return 0.5 * x * (1.0 + jnp.tanh(math.sqrt(2.0 / math.pi) * (x + 0.044715 * (x * x * x))))


def _tail_kernel(x_ref, y5_ref, yb_ref, ga_ref, gb_ref, gluw_ref, glub_ref, wa_ref, wb_ref,
                 wo_ref, nm_ref, w1_ref, w2_ref, nf_ref, o_ref, *, apply_final):
    ya = _gelu_tanh(y5_ref[...])
    ya = ya * _sigmoid(_dot(ya.astype(BF16), gluw_ref[...]) + glub_ref[...])
    ya = _dot(ya.astype(BF16), wa_ref[...])
    yb = _dot(yb_ref[...].astype(BF16), wb_ref[...])
    merged = _sigmoid(ga_ref[...]) * ya + _sigmoid(gb_ref[...]) * yb
    x1 = x_ref[...] + _dot(merged.astype(BF16), wo_ref[...])
    hb = _rms(x1, nm_ref[...]).astype(BF16)
    acc = x1
    d_ff = w1_ref.shape[-1]
    ff_block = 1024
    for k in range(d_ff // ff_block):
        cols = slice(k * ff_block, (k + 1) * ff_block)
        act = jnp.maximum(_dot(hb, w1_ref[:, cols]), 0.0)
        acc = acc + _dot((act * act).astype(BF16), w2_ref[cols, :])
    if apply_final:
        acc = _rms(acc, nf_ref[...])
    o_ref[...] = acc


def _tail(x2, y5, yb, ga, gb, gluw, glub, wa, wb, wo, nm, w1, w2, nf, apply_final):
    t, d = x2.shape
    tm = TOKEN_TILE
    row = lambda n: pl.BlockSpec((tm, n), lambda i: (i, 0))
    const = lambda arr: pl.BlockSpec(arr.shape, lambda i: (0, 0), pipeline_mode=pl.Buffered(1))
    consts = (gluw, glub, wa, wb, wo, nm, w1, w2, nf)
    return pl.pallas_call(
        functools.partial(_tail_kernel, apply_final=apply_final),
        grid=(t // tm,),
        in_specs=[row(d), row(y5.shape[-1]), row(yb.shape[-1]), row(d), row(d)]
        + [const(c) for c in consts],
        out_specs=row(d),
        out_shape=jax.ShapeDtypeStruct((t, d), F32),
        compiler_params=pltpu.CompilerParams(
            dimension_semantics=("arbitrary",), vmem_limit_bytes=VMEM_LIMIT),
        name="tail",
    )(x2, y5, yb, ga, gb, *consts)


def kernel(x, norm_mix, w_in, s5_a_re, s5_a_im, s5_log_dt, s5_b_re, s5_b_im, s5_c_re, s5_c_im,
           s5_d, s5_glu_w, s5_glu_b, w_branch_a, ssd_conv_w, ssd_conv_b, ssd_dt_bias, ssd_a_log,
           ssd_d, ssd_norm, w_branch_b, w_out, norm_mlp, w_ff1, w_ff2, norm_final):
    bsz, seq, d_model = x.shape
    depth = w_in.shape[0]
    s5_width = s5_glu_w.shape[-1]
    ssd_width = w_branch_b.shape[1]
    conv_dim = ssd_conv_w.shape[-1]
    heads = ssd_a_log.shape[-1]
    o_dt = s5_width + ssd_width + conv_dim
    o_ga = o_dt + heads
    assert o_dt % LANES == 0 and heads <= LANES and o_dt + LANES <= w_in.shape[-1]
    row = lambda v: v.reshape(1, -1).astype(F32)
    pad_lanes = lambda v: jnp.pad(v.reshape(1, -1).astype(F32), ((0, 0), (0, LANES - v.shape[-1])))

    x2 = x.reshape(bsz * seq, d_model)
    for i in range(depth):
        u, ga, gb, yb = _proj_ssd(
            x2.reshape(bsz, seq, d_model), row(norm_mix[i]),
            w_in[i].astype(BF16), w_in[i, :, o_ga:].astype(BF16),
            0.5 * ssd_conv_w[i], row(0.5 * ssd_conv_b[i]),
            pad_lanes(ssd_dt_bias[i]), pad_lanes(-jnp.exp(ssd_a_log[i]) * LOG2E),
            row(jnp.repeat(ssd_d[i], SSD_HEADDIM)), row(ssd_norm[i]), s5_width, d_model)

        kd, sbd, cod, are, aim = _s5_operators(s5_a_re[i], s5_a_im[i], s5_log_dt[i], s5_b_re[i],
                                               s5_b_im[i], s5_c_re[i], s5_c_im[i])
        y5 = _s5_branch(u, kd, sbd, cod, are, aim, s5_d[i].reshape(s5_width // LANES, 1, LANES))

        x2 = _tail(x2, y5.reshape(bsz * seq, s5_width), yb.reshape(bsz * seq, ssd_width),
                   ga.reshape(bsz * seq, d_model), gb.reshape(bsz * seq, d_model),
                   s5_glu_w[i].astype(BF16), row(s5_glu_b[i]), w_branch_a[i].astype(BF16),
                   w_branch_b[i].astype(BF16), w_out[i].astype(BF16), row(norm_mlp[i]),
                   w_ff1[i].astype(BF16), w_ff2[i].astype(BF16), row(norm_final),
                   apply_final=(i == depth - 1))
    return x2.reshape(bsz, seq, d_model)
```

```python
import functools
import math

import jax
import jax.numpy as jnp
from jax import lax
from jax.experimental import pallas as pl
from jax.experimental.pallas import tpu as pltpu

F32 = jnp.float32
BF16 = jnp.bfloat16

EPS = 1e-6
LANES = 128
SUBLANES = 8

S5_GROUP = 16
S5_STATE = 64
S5_Q = 8
S5_GROUPS_PER_TILE = LANES // S5_GROUP
S5_TT = 512
S5_ROW_PAD = SUBLANES

SSD_HEADDIM = 64
SSD_STATE = 128
SSD_CONV = 4
SSD_Q = 128
SSD_TT = 512
TABLES_AFTER_BLOCK = 2
LOG2E = math.log2(math.e)

TOKEN_TILE = 512
VMEM_LIMIT = 56 * 1024 * 1024


def _rms(x, g):
    return x * lax.rsqrt(jnp.mean(x * x, axis=-1, keepdims=True) + EPS) * g


def _sigmoid(x):
    return 1.0 / (1.0 + jnp.exp(-x))


def _dot(a, b):
    return jnp.dot(a, b, preferred_element_type=F32)


def _s5_operators(a_re, a_im, log_dt, b_re, b_im, c_re, c_im):
    q = S5_Q
    g, p = a_re.shape
    dt = jnp.exp(log_dt)[:, None]
    steps = jnp.arange(q + 1, dtype=F32)[:, None, None]
    mag = jnp.exp(a_re * dt * steps)
    ang = a_im * dt * steps
    pw_re = mag * jnp.cos(ang)
    pw_im = mag * jnp.sin(ang)
    nr = pw_re[1] - 1.0
    ni = pw_im[1]
    den = a_re * a_re + a_im * a_im
    cr = (nr * a_re + ni * a_im) / den
    ci = (ni * a_re - nr * a_im) / den
    nt = g // S5_GROUPS_PER_TILE
    g8 = S5_GROUPS_PER_TILE
    hh = S5_GROUP
    kq = q * LANES
    bt_re = b_re.transpose(0, 2, 1)
    bt_im = b_im.transpose(0, 2, 1)
    bb_re = cr[:, None, :] * bt_re - ci[:, None, :] * bt_im
    bb_im = cr[:, None, :] * bt_im + ci[:, None, :] * bt_re
    cp_re = c_re[None] * pw_re[:, :, None, :] - c_im[None] * pw_im[:, :, None, :]
    cp_im = c_re[None] * pw_im[:, :, None, :] + c_im[None] * pw_re[:, :, None, :]
    kj = jnp.sum(cp_re[:q, :, :, None, :] * bb_re[None, :, None, :, :]
                 - cp_im[:q, :, :, None, :] * bb_im[None, :, None, :, :], axis=-1)
    kd = kj.transpose(1, 3, 0, 2).reshape(nt, LANES, q * hh)
    rev_steps = (q - 1) - steps[:q]
    rev_mag = jnp.exp(a_re * dt * rev_steps)
    rev_ang = a_im * dt * rev_steps
    rev_re = (rev_mag * jnp.cos(rev_ang))[:, :, None, :]
    rev_im = (rev_mag * jnp.sin(rev_ang))[:, :, None, :]
    sb = jnp.concatenate([rev_re * bb_re[None] - rev_im * bb_im[None],
                          rev_re * bb_im[None] + rev_im * bb_re[None]], axis=-1)
    sbd = sb.reshape(q, nt, LANES, 2 * p).transpose(1, 0, 2, 3).reshape(nt, kq, 2 * p)
    ct_re = c_re.transpose(0, 2, 1)
    ct_im = c_im.transpose(0, 2, 1)
    co_re = jnp.concatenate([ct_re * pw_re[t + 1][:, :, None] - ct_im * pw_im[t + 1][:, :, None]
                             for t in range(q)], axis=-1)
    co_im = jnp.concatenate([-(ct_re * pw_im[t + 1][:, :, None] + ct_im * pw_re[t + 1][:, :, None])
                             for t in range(q)], axis=-1)
    cod = (jnp.stack([co_re, co_im], axis=0).reshape(2, nt, g8 * p, q * hh)
           .transpose(1, 0, 2, 3).reshape(nt, 2 * g8 * p, q * hh))
    are = pw_re[q].reshape(nt, 1, g8 * p)
    aim = pw_im[q].reshape(nt, 1, g8 * p)
    return kd.astype(BF16), sbd.astype(BF16), cod.astype(BF16), are, aim


def _field(idx, period, width):
    return (idx & (period - 1)) >> (width.bit_length() - 1)


def _s5_expand(compact, rep, row_grp, col_grp):
    wide = _dot(compact, rep)
    return jnp.where(row_grp == col_grp, wide, 0.0).astype(BF16)


def _s5_build_operators(kd_ref, sbd_ref, cod_ref, wi_ref, ws_ref, wo_ref):
    q = S5_Q
    hh = S5_GROUP
    gw = LANES // 2
    cw = q * gw
    p = sbd_ref.shape[-1] // 2
    sw = wo_ref.shape[1]
    iota = lambda shape, d: lax.broadcasted_iota(jnp.int32, shape, d)

    def rep_th(shift):
        r, c = iota((q * hh, cw), 0), iota((q * hh, cw), 1)
        return ((_field(r, q * hh, hh) + shift == _field(c, cw, gw))
                & (_field(r, hh, 1) == _field(c, hh, 1))).astype(BF16)

    r2, c2 = iota((2 * p, sw), 0), iota((2 * p, sw), 1)
    rep_sp = ((_field(r2, 2 * p, p) == _field(c2, sw, sw // 2))
              & (_field(r2, p, 1) == _field(c2, p, 1))).astype(BF16)
    chan_grp = lambda idx: _field(idx, gw, hh)
    state_grp = lambda idx: _field(idx, sw // 2, p)
    for hf in range(2):
        for j in range(q):
            wi_ref[hf, j * gw:(j + 1) * gw, :] = _s5_expand(
                kd_ref[0, hf * gw:(hf + 1) * gw, :], rep_th(j),
                chan_grp(iota((gw, cw), 0)), chan_grp(iota((gw, cw), 1)))
        sb_half = jnp.concatenate(
            [sbd_ref[0, j * LANES + hf * gw:j * LANES + (hf + 1) * gw, :] for j in range(q)], axis=0)
        ws_ref[hf] = _s5_expand(sb_half, rep_sp, chan_grp(iota((cw, sw), 0)), state_grp(iota((cw, sw), 1)))
        co_half = jnp.concatenate(
            [cod_ref[0, s * sw + hf * (sw // 2):s * sw + (hf + 1) * (sw // 2), :] for s in range(2)], axis=0)
        wo_ref[hf] = _s5_expand(co_half, rep_th(0), state_grp(iota((sw, cw), 0)), chan_grp(iota((sw, cw), 1)))


def _s5_kernel(u_ref, kd_ref, sbd_ref, cod_ref, are_ref, aim_ref, d_ref, y_ref,
               sre_ref, sim_ref, v_ref, sp_ref, yi_ref, wi_ref, ws_ref, wo_ref):
    q = S5_Q
    nb, tt, _ = u_ref.shape
    c = tt // q
    r = nb * c
    half = sre_ref.shape[-1]

    @pl.when(pl.program_id(1) == 0)
    def _():
        sre_ref[...] = jnp.zeros_like(sre_ref)
        sim_ref[...] = jnp.zeros_like(sim_ref)
        _s5_build_operators(kd_ref, sbd_ref, cod_ref, wi_ref, ws_ref, wo_ref)

    us = [u_ref[:, pl.ds(j, c, stride=q), :].reshape(r, LANES) for j in range(q)]

    low = lax.broadcasted_iota(jnp.int32, (r, LANES), 1) < LANES // 2
    swap = lambda x: pltpu.roll(x, LANES // 2, 1)
    halves = ([], [])
    for pair in range(q // 2):
        even, odd = us[2 * pair], us[2 * pair + 1]
        halves[0].append(jnp.where(low, even, swap(odd)).astype(BF16))
        halves[1].append(jnp.where(low, swap(even), odd).astype(BF16))
    a = [jnp.concatenate(slabs_h, axis=1) for slabs_h in halves]

    slabs = half // LANES
    per_half = slabs // 2
    pitch = v_ref.shape[1] // nb
    mxu = 2 * LANES
    for hf in range(2):
        v = _dot(a[hf], ws_ref[hf])
        for part in range(2):
            for i in range(per_half):
                cols = slice((part * per_half + i) * LANES, (part * per_half + i + 1) * LANES)
                n = part * slabs + hf * per_half + i
                for b in range(nb):
                    v_ref[n, b * pitch:b * pitch + c, :] = v[b * c:(b + 1) * c, cols]

    tiles_per_half = a[0].shape[-1] // mxu
    for hf in range(2):
        for n in range(tiles_per_half):
            yi_ref[hf * tiles_per_half + n] = _dot(
                a[hf][:, :(n + 1) * mxu], wi_ref[hf, :(n + 1) * mxu, n * mxu:(n + 1) * mxu])

    are = jnp.broadcast_to(are_ref[0], (nb, half))
    aim = jnp.broadcast_to(aim_ref[0], (nb, half))
    s_re = sre_ref[...]
    s_im = sim_ref[...]
    for k in range(c):
        step_rows = pl.ds(k, nb, stride=pitch)
        v_re = jnp.concatenate([v_ref[n, step_rows, :] for n in range(slabs)], axis=1)
        v_im = jnp.concatenate([v_ref[slabs + n, step_rows, :] for n in range(slabs)], axis=1)
        for n in range(slabs):
            sp_ref[n, step_rows, :] = s_re[:, n * LANES:(n + 1) * LANES]
            sp_ref[slabs + n, step_rows, :] = s_im[:, n * LANES:(n + 1) * LANES]
        n_re = are * s_re - aim * s_im + v_re
        n_im = are * s_im + aim * s_re + v_im
        s_re, s_im = n_re, n_im
    sre_ref[...] = s_re
    sim_ref[...] = s_im

    def entering_states(hf):
        ns = [part * slabs + hf * per_half + i for part in range(2) for i in range(per_half)]
        return jnp.concatenate(
            [jnp.concatenate([sp_ref[n, b * pitch:b * pitch + c, :] for b in range(nb)], axis=0)
             for n in ns], axis=1).astype(BF16)

    spb = [entering_states(hf) for hf in range(2)]
    d = d_ref[0]
    pairs_per_tile = mxu // LANES
    for n in range(tiles_per_half):
        y_tile = [yi_ref[hf * tiles_per_half + n] + _dot(spb[hf], wo_ref[hf, :, n * mxu:(n + 1) * mxu])
                  for hf in range(2)]
        for i in range(pairs_per_tile):
            pair = n * pairs_per_tile + i
            y0 = y_tile[0][:, i * LANES:(i + 1) * LANES]
            y1 = y_tile[1][:, i * LANES:(i + 1) * LANES]
            steps = ((2 * pair, jnp.where(low, y0, swap(y1))), (2 * pair + 1, jnp.where(low, swap(y0), y1)))
            for j, yj in steps:
                y_ref[:, pl.ds(j, c, stride=q), :] = (yj + d * us[j]).reshape(nb, c, LANES)


def _s5_branch(u3, kd, sbd, cod, are, aim, d):
    nb, seq, width = u3.shape
    nt = width // LANES
    tt = S5_TT
    kq = S5_Q * LANES
    half = are.shape[-1]
    r = nb * (tt // S5_Q + S5_ROW_PAD)
    tile = pl.BlockSpec((nb, tt, LANES), lambda l, t: (0, t, l))
    per_tile = lambda arr: pl.BlockSpec((1,) + arr.shape[1:], lambda l, t: (l, 0, 0))
    return pl.pallas_call(
        _s5_kernel,
        grid=(nt, seq // tt),
        in_specs=[tile] + [per_tile(arr) for arr in (kd, sbd, cod, are, aim, d)],
        out_specs=tile,
        out_shape=jax.ShapeDtypeStruct(u3.shape, F32),
        scratch_shapes=[pltpu.VMEM((nb, half), F32), pltpu.VMEM((nb, half), F32),
                        pltpu.VMEM((2 * half // LANES, r, LANES), F32),
                        pltpu.VMEM((2 * half // LANES, r, LANES), F32),
                        pltpu.VMEM((kq // (2 * LANES), nb * tt // S5_Q, 2 * LANES), F32),
                        pltpu.VMEM((2, kq // 2, kq // 2), BF16), pltpu.VMEM((2, kq // 2, half), BF16),
                        pltpu.VMEM((2, half, kq // 2), BF16)],
        compiler_params=pltpu.CompilerParams(
            dimension_semantics=("arbitrary", "arbitrary"), vmem_limit_bytes=VMEM_LIMIT),
        name="s5_branch",
    )(u3, kd, sbd, cod, are, aim, d)


def _split3(x):
    hi = x.astype(BF16)
    r1 = x - hi.astype(F32)
    mid = r1.astype(BF16)
    lo = (r1 - mid.astype(F32)).astype(BF16)
    return hi, mid, lo


def _project_block(hb_ref, w_ref, dst_ref, src_col, dst_col, n):
    dst_ref[:, dst_col:dst_col + n] = _dot(hb_ref[...], w_ref[:, src_col:src_col + n])


def _proj_ssd_kernel(x_ref, g_ref, w_ref, wg_ref, cw_ref, cb_ref, dtb_ref, a2_ref, dskip_ref,
                     ng_ref, u_ref, ga_ref, gb_ref, y_ref,
                     hb_ref, xc_ref, z_ref, acum_ref, acumt_ref, dtt_ref, wt_ref,
                     ys_ref, st_ref, *pad_refs):
    tt, width = y_ref.shape
    q = SSD_Q
    n_state = SSD_STATE
    slabs = len(pad_refs)
    pairs = width // LANES
    groups = (slabs - pairs) // 2
    halo = SUBLANES
    taps = cw_ref.shape[0]
    mxu = 2 * LANES
    o_z = u_ref.shape[-1]
    o_xbc = o_z + width
    o_dt = o_xbc + slabs * LANES

    @pl.when(pl.program_id(1) == 0)
    def _():
        for pad_ref in pad_refs:
            pad_ref[tt:tt + halo, :] = jnp.zeros((halo, LANES), F32)
        st_ref[...] = jnp.zeros_like(st_ref)

    dt_parts, xbc0_parts = [], []
    for ci in range(tt // q):
        rows = slice(ci * q, (ci + 1) * q)
        hb = _rms(x_ref[rows, :], g_ref[...]).astype(BF16)
        hb_ref[rows, :] = hb
        dt_parts.append(_dot(hb, w_ref[:, o_dt:o_dt + LANES]))
        xbc0_parts.append(_dot(hb, w_ref[:, o_xbc:o_xbc + mxu]))

    def project(src_col, n):
        return _dot(hb_ref[...], w_ref[:, src_col:src_col + n])

    row_i = lax.broadcasted_iota(jnp.int32, (q, q), 0)
    col_i = lax.broadcasted_iota(jnp.int32, (q, q), 1)
    causal = row_i >= col_i
    tri = causal.astype(BF16)
    lane = lax.broadcasted_iota(jnp.int32, (q, LANES), 1)
    first_head = lane < SSD_HEADDIM
    first_mask = first_head.astype(BF16)
    second_mask = 1.0 - first_mask

    v = jnp.concatenate(dt_parts, axis=0) + dtb_ref[...]
    dts = jnp.maximum(v, 0.0) + jnp.log1p(jnp.exp(-jnp.abs(v)))
    n_chunks = tt // q

    def decay_tables():
        for ci in range(n_chunks):
            rows = slice(ci * q, (ci + 1) * q)
            dtq = dts[rows]
            hi, mid, lo = _split3(dtq * a2_ref[...])
            parts = _dot(tri, jnp.concatenate([hi, mid, lo], axis=1))
            acum = (parts[:, :LANES] + parts[:, LANES:2 * LANES]) + parts[:, 2 * LANES:]
            acum_t = acum.T
            dt_t = dtq.T
            acum_ref[rows, :] = acum
            acumt_ref[ci] = acum_t
            dtt_ref[ci] = dt_t
            wt_ref[ci] = jnp.exp2(acum_t[:, q - 1:q] - acum_t) * dt_t

    def deferred(dst_ref, src_ref, src_col):
        return [functools.partial(_project_block, hb_ref, src_ref, dst_ref, src_col + c, c, mxu)
                for c in range(0, dst_ref.shape[-1], mxu)]

    z_blocks = deferred(z_ref, w_ref, o_z)
    other_blocks = (deferred(u_ref, w_ref, 0) + deferred(ga_ref, wg_ref, 0)
                    + deferred(gb_ref, wg_ref, ga_ref.shape[-1]))
    xbc_blocks = slabs * LANES // mxu
    assert len(z_blocks) <= xbc_blocks

    half = tt // 2
    for blk in range(xbc_blocks):
        res = project(o_xbc + blk * mxu, mxu) if blk else jnp.concatenate(xbc0_parts, axis=0)
        for i in range(mxu // LANES):
            n = blk * (mxu // LANES) + i
            cols = slice(n * LANES, (n + 1) * LANES)
            pad_ref = pad_refs[n]
            pad_ref[0:halo, :] = pad_ref[tt:tt + halo, :]
            pad_ref[halo:, :] = res[:, i * LANES:(i + 1) * LANES]
            for s in range(2):
                acc = cb_ref[:, cols]
                for k in range(taps):
                    start = halo - (taps - 1) + s + k
                    acc = acc + cw_ref[k:k + 1, cols] * pad_ref[pl.ds(start, half, stride=2), :]
                xc_ref[n, pl.ds(s, half, stride=2), :] = acc + acc * jnp.tanh(acc)
        if blk == TABLES_AFTER_BLOCK:
            decay_tables()
        if blk >= xbc_blocks - len(z_blocks):
            z_blocks[blk - (xbc_blocks - len(z_blocks))]()

    slots = n_chunks * pairs
    issue_at = {(i * slots) // len(other_blocks): blk for i, blk in enumerate(other_blocks)}
    assert len(issue_at) == len(other_blocks)

    for ci in range(n_chunks):
        rows = slice(ci * q, (ci + 1) * q)
        acum = acum_ref[rows, :]
        acum_t = acumt_ref[ci]
        dt_t = dtt_ref[ci]
        w_t = wt_ref[ci]
        for g in range(groups):
            bg = xc_ref[pairs + g, rows, :]
            cg = xc_ref[pairs + groups + g, rows, :].astype(BF16)
            cb = lax.dot_general(cg, bg.astype(BF16), (((1,), (1,)), ((), ())),
                                 preferred_element_type=F32)
            cbm = jnp.where(causal, cb, 0.0).astype(BF16)
            bgt = bg.T.astype(BF16)
            st_prevs = [st_ref[2 * g], st_ref[2 * g + 1]]
            y_offs = _dot(cg, jnp.concatenate([s.astype(BF16) for s in st_prevs], axis=1))
            for m in (2 * g, 2 * g + 1):
                if ci * pairs + m in issue_at:
                    issue_at[ci * pairs + m]()
                xs = xc_ref[m, rows, :]
                mats, wts, cols = [], [], []
                for h in (2 * m, 2 * m + 1):
                    col = jnp.broadcast_to(acum[:, h:h + 1], (q, q))
                    cols.append(col)
                    rowv = jnp.broadcast_to(acum_t[h:h + 1, :], (q, q))
                    dtrow = jnp.broadcast_to(dt_t[h:h + 1, :], (q, q)).astype(BF16)
                    decay = jnp.exp2(jnp.minimum(col - rowv, 0.0)).astype(BF16)
                    mats.append(cbm * decay * dtrow)
                    wts.append(bgt * jnp.broadcast_to(w_t[h:h + 1, :], (n_state, q)).astype(BF16))
                mcat = jnp.concatenate(mats, axis=1)
                wcat = jnp.concatenate(wts, axis=1)
                xsb = xs.astype(BF16)
                xbd = jnp.concatenate([xsb * first_mask, xsb * second_mask], axis=0)
                y_diag = _dot(mcat, xbd)
                st_new = _dot(wcat, xbd)
                st_prev = st_prevs[m - 2 * g]
                ecol = jnp.exp2(jnp.where(first_head, cols[0], cols[1]))
                y_off = y_offs[:, (m - 2 * g) * LANES:(m - 2 * g + 1) * LANES] * ecol
                st_ref[m] = st_prev * ecol[q - 1:q, :] + st_new
                ys_ref[rows, m * LANES:(m + 1) * LANES] = (
                    y_diag + y_off + xs * dskip_ref[:, m * LANES:(m + 1) * LANES])
        hz = 0.5 * z_ref[rows, :]
        yg = ys_ref[rows, :] * (hz + hz * jnp.tanh(hz))
        y_ref[rows, :] = _rms(yg, ng_ref[...])


def _proj_ssd(x3, g, w, wg, cw, cb, dtb, a2, dskip, ng, s5_width, d_model):
    nb, seq, d_in = x3.shape
    width = ng.shape[-1]
    cdim = cw.shape[-1]
    tt = SSD_TT
    slabs = cdim // LANES
    chunks = tt // SSD_Q
    tile = lambda n: pl.BlockSpec((None, tt, n), lambda b, t: (b, t, 0))
    const = lambda arr: pl.BlockSpec(arr.shape, lambda b, t: (0, 0), pipeline_mode=pl.Buffered(1))
    out_widths = (s5_width, d_model, d_model, width)
    return pl.pallas_call(
        _proj_ssd_kernel,
        grid=(nb, seq // tt),
        in_specs=[tile(d_in)] + [const(arr) for arr in (g, w, wg, cw, cb, dtb, a2, dskip, ng)],
        out_specs=[tile(n) for n in out_widths],
        out_shape=[jax.ShapeDtypeStruct((nb, seq, n), F32) for n in out_widths],
        scratch_shapes=[pltpu.VMEM((tt, d_in), BF16),
                        pltpu.VMEM((slabs, tt, LANES), F32),
                        pltpu.VMEM((tt, width), F32),
                        pltpu.VMEM((tt, LANES), F32),
                        pltpu.VMEM((chunks, SSD_Q, SSD_Q), F32),
                        pltpu.VMEM((chunks, SSD_Q, SSD_Q), F32),
                        pltpu.VMEM((chunks, SSD_Q, SSD_Q), F32),
                        pltpu.VMEM((tt, width), F32),
                        pltpu.VMEM((width // LANES, SSD_STATE, LANES), F32)]
        + [pltpu.VMEM((tt + SUBLANES, LANES), F32) for _ in range(slabs)],
        compiler_params=pltpu.CompilerParams(
            dimension_semantics=("arbitrary", "arbitrary"), vmem_limit_bytes=VMEM_LIMIT),
        name="proj_ssd",
    )(x3, g, w, wg, cw, cb, dtb, a2, dskip, ng)


def _gelu_tanh(x):
    return 0.5 * x * (1.0 + jnp.tanh(math.sqrt(2.0 / math.pi) * (x + 0.044715 * (x * x * x))))


def _tail_kernel(x_ref, y5_ref, yb_ref, ga_ref, gb_ref, gluw_ref, glub_ref, wa_ref, wb_ref,
                 wo_ref, nm_ref, w1_ref, w2_ref, nf_ref, o_ref, *, apply_final):
    ya = _gelu_tanh(y5_ref[...])
    ya = ya * _sigmoid(_dot(ya.astype(BF16), gluw_ref[...]) + glub_ref[...])
    ya = _dot(ya.astype(BF16), wa_ref[...])
    yb = _dot(yb_ref[...].astype(BF16), wb_ref[...])
    merged = _sigmoid(ga_ref[...]) * ya + _sigmoid(gb_ref[...]) * yb
    x1 = x_ref[...] + _dot(merged.astype(BF16), wo_ref[...])
    hb = _rms(x1, nm_ref[...]).astype(BF16)
    acc = x1
    d_ff = w1_ref.shape[-1]
    ff_block = 1024
    for k in range(d_ff // ff_block):
        cols = slice(k * ff_block, (k + 1) * ff_block)
        act = jnp.maximum(_dot(hb, w1_ref[:, cols]), 0.0)
        acc = acc + _dot((act * act).astype(BF16), w2_ref[cols, :])
    if apply_final:
        acc = _rms(acc, nf_ref[...])
    o_ref[...] = acc


def _tail(x2, y5, yb, ga, gb, gluw, glub, wa, wb, wo, nm, w1, w2, nf, apply_final):
    t, d = x2.shape
    tm = TOKEN_TILE
    row = lambda n: pl.BlockSpec((tm, n), lambda i: (i, 0))
    const = lambda arr: pl.BlockSpec(arr.shape, lambda i: (0, 0), pipeline_mode=pl.Buffered(1))
    consts = (gluw, glub, wa, wb, wo, nm, w1, w2, nf)
    return pl.pallas_call(
        functools.partial(_tail_kernel, apply_final=apply_final),
        grid=(t // tm,),
        in_specs=[row(d), row(y5.shape[-1]), row(yb.shape[-1]), row(d), row(d)]
        + [const(c) for c in consts],
        out_specs=row(d),
        out_shape=jax.ShapeDtypeStruct((t, d), F32),
        compiler_params=pltpu.CompilerParams(
            dimension_semantics=("arbitrary",), vmem_limit_bytes=VMEM_LIMIT),
        name="tail",
    )(x2, y5, yb, ga, gb, *consts)


def kernel(x, norm_mix, w_in, s5_a_re, s5_a_im, s5_log_dt, s5_b_re, s5_b_im, s5_c_re, s5_c_im,
           s5_d, s5_glu_w, s5_glu_b, w_branch_a, ssd_conv_w, ssd_conv_b, ssd_dt_bias, ssd_a_log,
           ssd_d, ssd_norm, w_branch_b, w_out, norm_mlp, w_ff1, w_ff2, norm_final):
    bsz, seq, d_model = x.shape
    depth = w_in.shape[0]
    s5_width = s5_glu_w.shape[-1]
    ssd_width = w_branch_b.shape[1]
    conv_dim = ssd_conv_w.shape[-1]
    heads = ssd_a_log.shape[-1]
    o_dt = s5_width + ssd_width + conv_dim
    o_ga = o_dt + heads
    assert o_dt % LANES == 0 and heads <= LANES and o_dt + LANES <= w_in.shape[-1]
    row = lambda v: v.reshape(1, -1).astype(F32)
    pad_lanes = lambda v: jnp.pad(v.reshape(1, -1).astype(F32), ((0, 0), (0, LANES - v.shape[-1])))

    x2 = x.reshape(bsz * seq, d_model)
    for i in range(depth):
        u, ga, gb, yb = _proj_ssd(
            x2.reshape(bsz, seq, d_model), row(norm_mix[i]),
            w_in[i].astype(BF16), w_in[i, :, o_ga:].astype(BF16),
            0.5 * ssd_conv_w[i], row(0.5 * ssd_conv_b[i]),
            pad_lanes(ssd_dt_bias[i]), pad_lanes(-jnp.exp(ssd_a_log[i]) * LOG2E),
            row(jnp.repeat(ssd_d[i], SSD_HEADDIM)), row(ssd_norm[i]), s5_width, d_model)

        kd, sbd, cod, are, aim = _s5_operators(s5_a_re[i], s5_a_im[i], s5_log_dt[i], s5_b_re[i],
                                               s5_b_im[i], s5_c_re[i], s5_c_im[i])
        y5 = _s5_branch(u, kd, sbd, cod, are, aim, s5_d[i].reshape(s5_width // LANES, 1, LANES))

        x2 = _tail(x2, y5.reshape(bsz * seq, s5_width), yb.reshape(bsz * seq, ssd_width),
                   ga.reshape(bsz * seq, d_model), gb.reshape(bsz * seq, d_model),
                   s5_glu_w[i].astype(BF16), row(s5_glu_b[i]), w_branch_a[i].astype(BF16),
                   w_branch_b[i].astype(BF16), w_out[i].astype(BF16), row(norm_mlp[i]),
                   w_ff1[i].astype(BF16), w_ff2[i].astype(BF16), row(norm_final),
                   apply_final=(i == depth - 1))
    return x2.reshape(bsz, seq, d_model)
```

```python
import functools
import math

import jax
import jax.numpy as jnp
from jax import lax
from jax.experimental import pallas as pl
from jax.experimental.pallas import tpu as pltpu

F32 = jnp.float32
BF16 = jnp.bfloat16

EPS = 1e-6
LANES = 128
SUBLANES = 8
MXU_WIDTH = 2 * LANES

S5_GROUP = 16
S5_Q = 8
S5_GROUPS_PER_TILE = LANES // S5_GROUP
S5_TT = 512
S5_ROW_PAD = SUBLANES

SSD_HEADDIM = 64
SSD_STATE = 128
SSD_Q = 128
SSD_TT = 512
TABLES_AFTER_BLOCK = 2
LOG2E = math.log2(math.e)

TOKEN_TILE = 512
FF_BLOCK = 1024
VMEM_LIMIT = 56 * 1024 * 1024


def _rms(x, g):
    return x * lax.rsqrt(jnp.mean(x * x, axis=-1, keepdims=True) + EPS) * g


def _sigmoid(x):
    return 1.0 / (1.0 + jnp.exp(-x))


def _dot(a, b):
    return jnp.dot(a, b, preferred_element_type=F32)


def _s5_operators(a_re, a_im, log_dt, b_re, b_im, c_re, c_im):
    q = S5_Q
    g, p = a_re.shape
    dt = jnp.exp(log_dt)[:, None]
    steps = jnp.arange(q + 1, dtype=F32)[:, None, None]
    mag = jnp.exp(a_re * dt * steps)
    ang = a_im * dt * steps
    pw_re = mag * jnp.cos(ang)
    pw_im = mag * jnp.sin(ang)
    nr = pw_re[1] - 1.0
    ni = pw_im[1]
    den = a_re * a_re + a_im * a_im
    cr = (nr * a_re + ni * a_im) / den
    ci = (ni * a_re - nr * a_im) / den
    nt = g // S5_GROUPS_PER_TILE
    g8 = S5_GROUPS_PER_TILE
    hh = S5_GROUP
    kq = q * LANES
    bt_re = b_re.transpose(0, 2, 1)
    bt_im = b_im.transpose(0, 2, 1)
    bb_re = cr[:, None, :] * bt_re - ci[:, None, :] * bt_im
    bb_im = cr[:, None, :] * bt_im + ci[:, None, :] * bt_re
    cp_re = c_re[None] * pw_re[:, :, None, :] - c_im[None] * pw_im[:, :, None, :]
    cp_im = c_re[None] * pw_im[:, :, None, :] + c_im[None] * pw_re[:, :, None, :]
    kj = jnp.sum(cp_re[:q, :, :, None, :] * bb_re[None, :, None, :, :]
                 - cp_im[:q, :, :, None, :] * bb_im[None, :, None, :, :], axis=-1)
    kd = kj.transpose(1, 3, 0, 2).reshape(nt, LANES, q * hh)
    rev_steps = (q - 1) - steps[:q]
    rev_mag = jnp.exp(a_re * dt * rev_steps)
    rev_ang = a_im * dt * rev_steps
    rev_re = (rev_mag * jnp.cos(rev_ang))[:, :, None, :]
    rev_im = (rev_mag * jnp.sin(rev_ang))[:, :, None, :]
    sb = jnp.concatenate([rev_re * bb_re[None] - rev_im * bb_im[None],
                          rev_re * bb_im[None] + rev_im * bb_re[None]], axis=-1)
    sbd = sb.reshape(q, nt, LANES, 2 * p).transpose(1, 0, 2, 3).reshape(nt, kq, 2 * p)
    ct_re = c_re.transpose(0, 2, 1)
    ct_im = c_im.transpose(0, 2, 1)
    co_re = jnp.concatenate([ct_re * pw_re[t + 1][:, :, None] - ct_im * pw_im[t + 1][:, :, None]
                             for t in range(q)], axis=-1)
    co_im = jnp.concatenate([-(ct_re * pw_im[t + 1][:, :, None] + ct_im * pw_re[t + 1][:, :, None])
                             for t in range(q)], axis=-1)
    cod = (jnp.stack([co_re, co_im], axis=0).reshape(2, nt, g8 * p, q * hh)
           .transpose(1, 0, 2, 3).reshape(nt, 2 * g8 * p, q * hh))
    are = pw_re[q].reshape(nt, 1, g8 * p)
    aim = pw_im[q].reshape(nt, 1, g8 * p)
    return kd.astype(BF16), sbd.astype(BF16), cod.astype(BF16), are, aim


def _field(idx, period, width):
    return (idx & (period - 1)) >> (width.bit_length() - 1)


def _s5_expand(compact, rep, row_grp, col_grp):
    wide = _dot(compact, rep)
    return jnp.where(row_grp == col_grp, wide, 0.0).astype(BF16)


def _s5_build_operators(kd_ref, sbd_ref, cod_ref, wi_ref, ws_ref, wo_ref):
    q = S5_Q
    hh = S5_GROUP
    gw = LANES // 2
    cw = q * gw
    p = sbd_ref.shape[-1] // 2
    sw = wo_ref.shape[1]
    iota = lambda shape, d: lax.broadcasted_iota(jnp.int32, shape, d)

    def rep_th(shift):
        r, c = iota((q * hh, cw), 0), iota((q * hh, cw), 1)
        return ((_field(r, q * hh, hh) + shift == _field(c, cw, gw))
                & (_field(r, hh, 1) == _field(c, hh, 1))).astype(BF16)

    r2, c2 = iota((2 * p, sw), 0), iota((2 * p, sw), 1)
    rep_sp = ((_field(r2, 2 * p, p) == _field(c2, sw, sw // 2))
              & (_field(r2, p, 1) == _field(c2, p, 1))).astype(BF16)
    chan_grp = lambda idx: _field(idx, gw, hh)
    state_grp = lambda idx: _field(idx, sw // 2, p)
    for hf in range(2):
        for j in range(q):
            wi_ref[hf, j * gw:(j + 1) * gw, :] = _s5_expand(
                kd_ref[0, hf * gw:(hf + 1) * gw, :], rep_th(j),
                chan_grp(iota((gw, cw), 0)), chan_grp(iota((gw, cw), 1)))
        sb_half = jnp.concatenate(
            [sbd_ref[0, j * LANES + hf * gw:j * LANES + (hf + 1) * gw, :] for j in range(q)], axis=0)
        ws_ref[hf] = _s5_expand(sb_half, rep_sp, chan_grp(iota((cw, sw), 0)), state_grp(iota((cw, sw), 1)))
        co_half = jnp.concatenate(
            [cod_ref[0, s * sw + hf * (sw // 2):s * sw + (hf + 1) * (sw // 2), :] for s in range(2)], axis=0)
        wo_ref[hf] = _s5_expand(co_half, rep_th(0), state_grp(iota((sw, cw), 0)), chan_grp(iota((sw, cw), 1)))


def _s5_kernel(u_ref, kd_ref, sbd_ref, cod_ref, are_ref, aim_ref, d_ref, y_ref,
               sre_ref, sim_ref, v_ref, sp_ref, yi_ref, wi_ref, ws_ref, wo_ref):
    q = S5_Q
    nb, tt, _ = u_ref.shape
    c = tt // q
    r = nb * c
    half = sre_ref.shape[-1]

    @pl.when(pl.program_id(1) == 0)
    def _():
        sre_ref[...] = jnp.zeros_like(sre_ref)
        sim_ref[...] = jnp.zeros_like(sim_ref)
        _s5_build_operators(kd_ref, sbd_ref, cod_ref, wi_ref, ws_ref, wo_ref)

    us = [u_ref[:, pl.ds(j, c, stride=q), :].reshape(r, LANES) for j in range(q)]

    low = lax.broadcasted_iota(jnp.int32, (r, LANES), 1) < LANES // 2
    swap = lambda x: pltpu.roll(x, LANES // 2, 1)
    halves = ([], [])
    for pair in range(q // 2):
        even, odd = us[2 * pair], us[2 * pair + 1]
        halves[0].append(jnp.where(low, even, swap(odd)).astype(BF16))
        halves[1].append(jnp.where(low, swap(even), odd).astype(BF16))
    a = [jnp.concatenate(slabs_h, axis=1) for slabs_h in halves]

    slabs = half // LANES
    per_half = slabs // 2
    pitch = v_ref.shape[1] // nb
    mxu = MXU_WIDTH
    for hf in range(2):
        v = _dot(a[hf], ws_ref[hf])
        for part in range(2):
            for i in range(per_half):
                cols = slice((part * per_half + i) * LANES, (part * per_half + i + 1) * LANES)
                n = part * slabs + hf * per_half + i
                for b in range(nb):
                    v_ref[n, b * pitch:b * pitch + c, :] = v[b * c:(b + 1) * c, cols]

    tiles_per_half = a[0].shape[-1] // mxu
    for hf in range(2):
        for n in range(tiles_per_half):
            yi_ref[hf * tiles_per_half + n] = _dot(
                a[hf][:, :(n + 1) * mxu], wi_ref[hf, :(n + 1) * mxu, n * mxu:(n + 1) * mxu])

    are = jnp.broadcast_to(are_ref[0], (nb, half))
    aim = jnp.broadcast_to(aim_ref[0], (nb, half))
    s_re = sre_ref[...]
    s_im = sim_ref[...]
    for k in range(c):
        step_rows = pl.ds(k, nb, stride=pitch)
        v_re = jnp.concatenate([v_ref[n, step_rows, :] for n in range(slabs)], axis=1)
        v_im = jnp.concatenate([v_ref[slabs + n, step_rows, :] for n in range(slabs)], axis=1)
        for n in range(slabs):
            sp_ref[n, step_rows, :] = s_re[:, n * LANES:(n + 1) * LANES]
            sp_ref[slabs + n, step_rows, :] = s_im[:, n * LANES:(n + 1) * LANES]
        n_re = are * s_re - aim * s_im + v_re
        n_im = are * s_im + aim * s_re + v_im
        s_re, s_im = n_re, n_im
    sre_ref[...] = s_re
    sim_ref[...] = s_im

    def entering_states(hf):
        ns = [part * slabs + hf * per_half + i for part in range(2) for i in range(per_half)]
        return jnp.concatenate(
            [jnp.concatenate([sp_ref[n, b * pitch:b * pitch + c, :] for b in range(nb)], axis=0)
             for n in ns], axis=1).astype(BF16)

    spb = [entering_states(hf) for hf in range(2)]
    d = d_ref[0]
    pairs_per_tile = mxu // LANES
    for n in range(tiles_per_half):
        y_tile = [yi_ref[hf * tiles_per_half + n] + _dot(spb[hf], wo_ref[hf, :, n * mxu:(n + 1) * mxu])
                  for hf in range(2)]
        for i in range(pairs_per_tile):
            pair = n * pairs_per_tile + i
            y0 = y_tile[0][:, i * LANES:(i + 1) * LANES]
            y1 = y_tile[1][:, i * LANES:(i + 1) * LANES]
            steps = ((2 * pair, jnp.where(low, y0, swap(y1))), (2 * pair + 1, jnp.where(low, swap(y0), y1)))
            for j, yj in steps:
                y_ref[:, pl.ds(j, c, stride=q), :] = (yj + d * us[j]).reshape(nb, c, LANES)


def _s5_branch(u3, kd, sbd, cod, are, aim, d):
    nb, seq, width = u3.shape
    nt = width // LANES
    tt = S5_TT
    kq = S5_Q * LANES
    half = are.shape[-1]
    r = nb * (tt // S5_Q + S5_ROW_PAD)
    tile = pl.BlockSpec((nb, tt, LANES), lambda l, t: (0, t, l))
    per_tile = lambda arr: pl.BlockSpec((1,) + arr.shape[1:], lambda l, t: (l, 0, 0))
    return pl.pallas_call(
        _s5_kernel,
        grid=(nt, seq // tt),
        in_specs=[tile] + [per_tile(arr) for arr in (kd, sbd, cod, are, aim, d)],
        out_specs=tile,
        out_shape=jax.ShapeDtypeStruct(u3.shape, F32),
        scratch_shapes=[pltpu.VMEM((nb, half), F32), pltpu.VMEM((nb, half), F32),
                        pltpu.VMEM((2 * half // LANES, r, LANES), F32),
                        pltpu.VMEM((2 * half // LANES, r, LANES), F32),
                        pltpu.VMEM((kq // MXU_WIDTH, nb * tt // S5_Q, MXU_WIDTH), F32),
                        pltpu.VMEM((2, kq // 2, kq // 2), BF16), pltpu.VMEM((2, kq // 2, half), BF16),
                        pltpu.VMEM((2, half, kq // 2), BF16)],
        compiler_params=pltpu.CompilerParams(
            dimension_semantics=("arbitrary", "arbitrary"), vmem_limit_bytes=VMEM_LIMIT),
        name="s5_branch",
    )(u3, kd, sbd, cod, are, aim, d)


def _split3(x):
    hi = x.astype(BF16)
    r1 = x - hi.astype(F32)
    mid = r1.astype(BF16)
    lo = (r1 - mid.astype(F32)).astype(BF16)
    return hi, mid, lo


def _project_block(hb_ref, w_ref, dst_ref, src_col, dst_col, n):
    dst_ref[:, dst_col:dst_col + n] = _dot(hb_ref[...], w_ref[:, src_col:src_col + n])


def _proj_ssd_kernel(x_ref, g_ref, w_ref, wg_ref, cw_ref, cb_ref, dtb_ref, a2_ref, dskip_ref,
                     ng_ref, u_ref, ga_ref, gb_ref, y_ref,
                     hb_ref, pad_ref, xc_ref, z_ref, acum_ref, acumt_ref, dtt_ref, wt_ref,
                     ys_ref, st_ref):
    tt, width = y_ref.shape
    q = SSD_Q
    n_state = SSD_STATE
    slabs = pad_ref.shape[0]
    pairs = width // LANES
    groups = (slabs - pairs) // 2
    halo = SUBLANES
    taps = cw_ref.shape[0]
    mxu = MXU_WIDTH
    o_z = u_ref.shape[-1]
    o_xbc = o_z + width
    o_dt = o_xbc + slabs * LANES

    @pl.when(pl.program_id(1) == 0)
    def _():
        pad_ref[:, tt:tt + halo, :] = jnp.zeros((slabs, halo, LANES), F32)
        st_ref[...] = jnp.zeros_like(st_ref)

    dt_parts, xbc0_parts = [], []
    for ci in range(tt // q):
        rows = slice(ci * q, (ci + 1) * q)
        hb = _rms(x_ref[rows, :], g_ref[...]).astype(BF16)
        hb_ref[rows, :] = hb
        dt_parts.append(_dot(hb, w_ref[:, o_dt:o_dt + LANES]))
        xbc0_parts.append(_dot(hb, w_ref[:, o_xbc:o_xbc + mxu]))

    def project(src_col, n):
        return _dot(hb_ref[...], w_ref[:, src_col:src_col + n])

    row_i = lax.broadcasted_iota(jnp.int32, (q, q), 0)
    col_i = lax.broadcasted_iota(jnp.int32, (q, q), 1)
    causal = row_i >= col_i
    tri = causal.astype(BF16)
    lane = lax.broadcasted_iota(jnp.int32, (q, LANES), 1)
    first_head = lane < SSD_HEADDIM
    first_mask = first_head.astype(BF16)
    second_mask = 1.0 - first_mask

    v = jnp.concatenate(dt_parts, axis=0) + dtb_ref[...]
    dts = jnp.maximum(v, 0.0) + jnp.log1p(jnp.exp(-jnp.abs(v)))
    n_chunks = tt // q

    def decay_tables():
        for ci in range(n_chunks):
            rows = slice(ci * q, (ci + 1) * q)
            dtq = dts[rows]
            hi, mid, lo = _split3(dtq * a2_ref[...])
            parts = _dot(tri, jnp.concatenate([hi, mid, lo], axis=1))
            acum = (parts[:, :LANES] + parts[:, LANES:2 * LANES]) + parts[:, 2 * LANES:]
            acum_t = acum.T
            dt_t = dtq.T
            acum_ref[rows, :] = acum
            acumt_ref[ci] = acum_t
            dtt_ref[ci] = dt_t
            wt_ref[ci] = jnp.exp2(acum_t[:, q - 1:q] - acum_t) * dt_t

    def deferred(dst_ref, src_ref, src_col):
        return [functools.partial(_project_block, hb_ref, src_ref, dst_ref, src_col + c, c, mxu)
                for c in range(0, dst_ref.shape[-1], mxu)]

    z_blocks = deferred(z_ref, w_ref, o_z)
    other_blocks = (deferred(u_ref, w_ref, 0) + deferred(ga_ref, wg_ref, 0)
                    + deferred(gb_ref, wg_ref, ga_ref.shape[-1]))
    xbc_blocks = slabs * LANES // mxu
    assert len(z_blocks) <= xbc_blocks

    half = tt // 2
    for blk in range(xbc_blocks):
        res = project(o_xbc + blk * mxu, mxu) if blk else jnp.concatenate(xbc0_parts, axis=0)
        for i in range(mxu // LANES):
            n = blk * (mxu // LANES) + i
            cols = slice(n * LANES, (n + 1) * LANES)
            pad_ref[n, 0:halo, :] = pad_ref[n, tt:tt + halo, :]
            pad_ref[n, halo:, :] = res[:, i * LANES:(i + 1) * LANES]
            for s in range(2):
                acc = cb_ref[:, cols]
                for k in range(taps):
                    start = halo - (taps - 1) + s + k
                    acc = acc + cw_ref[k:k + 1, cols] * pad_ref[n, pl.ds(start, half, stride=2), :]
                xc_ref[n, pl.ds(s, half, stride=2), :] = acc + acc * jnp.tanh(acc)
        if blk == TABLES_AFTER_BLOCK:
            decay_tables()
        if blk >= xbc_blocks - len(z_blocks):
            z_blocks[blk - (xbc_blocks - len(z_blocks))]()

    slots = n_chunks * pairs
    issue_at = {(i * slots) // len(other_blocks): blk for i, blk in enumerate(other_blocks)}
    assert len(issue_at) == len(other_blocks)

    for ci in range(n_chunks):
        rows = slice(ci * q, (ci + 1) * q)
        acum = acum_ref[rows, :]
        acum_t = acumt_ref[ci]
        dt_t = dtt_ref[ci]
        w_t = wt_ref[ci]
        for g in range(groups):
            bg = xc_ref[pairs + g, rows, :]
            cg = xc_ref[pairs + groups + g, rows, :].astype(BF16)
            cb = lax.dot_general(cg, bg.astype(BF16), (((1,), (1,)), ((), ())),
                                 preferred_element_type=F32)
            cbm = jnp.where(causal, cb, 0.0).astype(BF16)
            bgt = bg.T.astype(BF16)
            st_prevs = [st_ref[2 * g], st_ref[2 * g + 1]]
            y_offs = _dot(cg, jnp.concatenate([s.astype(BF16) for s in st_prevs], axis=1))
            for m in (2 * g, 2 * g + 1):
                if ci * pairs + m in issue_at:
                    issue_at[ci * pairs + m]()
                xs = xc_ref[m, rows, :]
                mats, wts, cols = [], [], []
                for h in (2 * m, 2 * m + 1):
                    col = jnp.broadcast_to(acum[:, h:h + 1], (q, q))
                    cols.append(col)
                    rowv = jnp.broadcast_to(acum_t[h:h + 1, :], (q, q))
                    dtrow = jnp.broadcast_to(dt_t[h:h + 1, :], (q, q)).astype(BF16)
                    decay = jnp.exp2(jnp.minimum(col - rowv, 0.0)).astype(BF16)
                    mats.append(cbm * decay * dtrow)
                    wts.append(bgt * jnp.broadcast_to(w_t[h:h + 1, :], (n_state, q)).astype(BF16))
                mcat = jnp.concatenate(mats, axis=1)
                wcat = jnp.concatenate(wts, axis=1)
                xsb = xs.astype(BF16)
                xbd = jnp.concatenate([xsb * first_mask, xsb * second_mask], axis=0)
                y_diag = _dot(mcat, xbd)
                st_new = _dot(wcat, xbd)
                st_prev = st_prevs[m - 2 * g]
                ecol = jnp.exp2(jnp.where(first_head, cols[0], cols[1]))
                y_off = y_offs[:, (m - 2 * g) * LANES:(m - 2 * g + 1) * LANES] * ecol
                st_ref[m] = st_prev * ecol[q - 1:q, :] + st_new
                ys_ref[rows, m * LANES:(m + 1) * LANES] = (
                    y_diag + y_off + xs * dskip_ref[:, m * LANES:(m + 1) * LANES])
        hz = 0.5 * z_ref[rows, :]
        yg = ys_ref[rows, :] * (hz + hz * jnp.tanh(hz))
        y_ref[rows, :] = _rms(yg, ng_ref[...])


def _proj_ssd(x3, g, w, wg, cw, cb, dtb, a2, dskip, ng, s5_width, d_model):
    nb, seq, d_in = x3.shape
    width = ng.shape[-1]
    cdim = cw.shape[-1]
    tt = SSD_TT
    slabs = cdim // LANES
    chunks = tt // SSD_Q
    tile = lambda n: pl.BlockSpec((None, tt, n), lambda b, t: (b, t, 0))
    const = lambda arr: pl.BlockSpec(arr.shape, lambda b, t: (0, 0), pipeline_mode=pl.Buffered(1))
    out_widths = (s5_width, d_model, d_model, width)
    return pl.pallas_call(
        _proj_ssd_kernel,
        grid=(nb, seq // tt),
        in_specs=[tile(d_in)] + [const(arr) for arr in (g, w, wg, cw, cb, dtb, a2, dskip, ng)],
        out_specs=[tile(n) for n in out_widths],
        out_shape=[jax.ShapeDtypeStruct((nb, seq, n), F32) for n in out_widths],
        scratch_shapes=[pltpu.VMEM((tt, d_in), BF16),
                        pltpu.VMEM((slabs, tt + SUBLANES, LANES), F32),
                        pltpu.VMEM((slabs, tt, LANES), F32),
                        pltpu.VMEM((tt, width), F32),
                        pltpu.VMEM((tt, LANES), F32),
                        pltpu.VMEM((chunks, SSD_Q, SSD_Q), F32),
                        pltpu.VMEM((chunks, SSD_Q, SSD_Q), F32),
                        pltpu.VMEM((chunks, SSD_Q, SSD_Q), F32),
                        pltpu.VMEM((tt, width), F32),
                        pltpu.VMEM((width // LANES, SSD_STATE, LANES), F32)],
        compiler_params=pltpu.CompilerParams(
            dimension_semantics=("arbitrary", "arbitrary"), vmem_limit_bytes=VMEM_LIMIT),
        name="proj_ssd",
    )(x3, g, w, wg, cw, cb, dtb, a2, dskip, ng)


def _gelu_tanh(x):
    return 0.5 * x * (1.0 + jnp.tanh(math.sqrt(2.0 / math.pi) * (x + 0.044715 * (x * x * x))))


def _tail_kernel(x_ref, y5_ref, yb_ref, ga_ref, gb_ref, gluw_ref, glub_ref, wa_ref, wb_ref,
                 wo_ref, nm_ref, w1_ref, w2_ref, nf_ref, o_ref, *, apply_final):
    ya = _gelu_tanh(y5_ref[...])
    ya = ya * _sigmoid(_dot(ya.astype(BF16), gluw_ref[...]) + glub_ref[...])
    ya = _dot(ya.astype(BF16), wa_ref[...])
    yb = _dot(yb_ref[...].astype(BF16), wb_ref[...])
    merged = _sigmoid(ga_ref[...]) * ya + _sigmoid(gb_ref[...]) * yb
    x1 = x_ref[...] + _dot(merged.astype(BF16), wo_ref[...])
    hb = _rms(x1, nm_ref[...]).astype(BF16)
    acc = x1
    d_ff = w1_ref.shape[-1]
    for k in range(d_ff // FF_BLOCK):
        cols = slice(k * FF_BLOCK, (k + 1) * FF_BLOCK)
        act = jnp.maximum(_dot(hb, w1_ref[:, cols]), 0.0)
        acc = acc + _dot((act * act).astype(BF16), w2_ref[cols, :])
    if apply_final:
        acc = _rms(acc, nf_ref[...])
    o_ref[...] = acc


def _tail(x2, y5, yb, ga, gb, gluw, glub, wa, wb, wo, nm, w1, w2, nf, apply_final):
    t, d = x2.shape
    tm = TOKEN_TILE
    row = lambda n: pl.BlockSpec((tm, n), lambda i: (i, 0))
    const = lambda arr: pl.BlockSpec(arr.shape, lambda i: (0, 0), pipeline_mode=pl.Buffered(1))
    consts = (gluw, glub, wa, wb, wo, nm, w1, w2, nf)
    return pl.pallas_call(
        functools.partial(_tail_kernel, apply_final=apply_final),
        grid=(t // tm,),
        in_specs=[row(d), row(y5.shape[-1]), row(yb.shape[-1]), row(d), row(d)]
        + [const(c) for c in consts],
        out_specs=row(d),
        out_shape=jax.ShapeDtypeStruct((t, d), F32),
        compiler_params=pltpu.CompilerParams(
            dimension_semantics=("arbitrary",), vmem_limit_bytes=VMEM_LIMIT),
        name="tail",
    )(x2, y5, yb, ga, gb, *consts)


def kernel(x, norm_mix, w_in, s5_a_re, s5_a_im, s5_log_dt, s5_b_re, s5_b_im, s5_c_re, s5_c_im,
           s5_d, s5_glu_w, s5_glu_b, w_branch_a, ssd_conv_w, ssd_conv_b, ssd_dt_bias, ssd_a_log,
           ssd_d, ssd_norm, w_branch_b, w_out, norm_mlp, w_ff1, w_ff2, norm_final):
    bsz, seq, d_model = x.shape
    depth = w_in.shape[0]
    s5_width = s5_glu_w.shape[-1]
    ssd_width = w_branch_b.shape[1]
    conv_dim = ssd_conv_w.shape[-1]
    heads = ssd_a_log.shape[-1]
    o_dt = s5_width + ssd_width + conv_dim
    o_ga = o_dt + heads
    assert o_dt % LANES == 0 and heads <= LANES and o_dt + LANES <= w_in.shape[-1]
    row = lambda v: v.reshape(1, -1).astype(F32)
    pad_lanes = lambda v: jnp.pad(v.reshape(1, -1).astype(F32), ((0, 0), (0, LANES - v.shape[-1])))

    x2 = x.reshape(bsz * seq, d_model)
    for i in range(depth):
        w_bf = w_in[i].astype(BF16)
        u, ga, gb, yb = _proj_ssd(
            x2.reshape(bsz, seq, d_model), row(norm_mix[i]), w_bf, w_bf[:, o_ga:],
            0.5 * ssd_conv_w[i], row(0.5 * ssd_conv_b[i]),
            pad_lanes(ssd_dt_bias[i]), pad_lanes(-jnp.exp(ssd_a_log[i]) * LOG2E),
            row(jnp.repeat(ssd_d[i], SSD_HEADDIM)), row(ssd_norm[i]), s5_width, d_model)

        kd, sbd, cod, are, aim = _s5_operators(s5_a_re[i], s5_a_im[i], s5_log_dt[i], s5_b_re[i],
                                               s5_b_im[i], s5_c_re[i], s5_c_im[i])
        y5 = _s5_branch(u, kd, sbd, cod, are, aim, s5_d[i].reshape(s5_width // LANES, 1, LANES))

        x2 = _tail(x2, y5.reshape(bsz * seq, s5_width), yb.reshape(bsz * seq, ssd_width),
                   ga.reshape(bsz * seq, d_model), gb.reshape(bsz * seq, d_model),
                   s5_glu_w[i].astype(BF16), row(s5_glu_b[i]), w_branch_a[i].astype(BF16),
                   w_branch_b[i].astype(BF16), w_out[i].astype(BF16), row(norm_mlp[i]),
                   w_ff1[i].astype(BF16), w_ff2[i].astype(BF16), row(norm_final),
                   apply_final=(i == depth - 1))
    return x2.reshape(bsz, seq, d_model)
```
